```python
import jax, jax.numpy as jnp
from jax import lax
import numpy as np

D_MODEL = 1024
BATCH = 4
SEQ = 8192
DEPTH = 4

HEAD_DIM = 64
A_HEADS = 8
A_KV_HEADS = 2
B_HEADS = 8
B_KV_HEADS = 2
WINDOW = 128
BLOCK = 128
ROPE_THETA = 10000.0
GRID_W = 64
SGU_WIDTH = D_MODEL
SGU_GROUPS = 8
SGU_CHUNK = 128
D_FF = 4 * D_MODEL
EPS = 1e-6
N_ATT_LAYERS = (DEPTH + 1) // 2
N_SGU_LAYERS = DEPTH // 2

A_Q = A_HEADS * HEAD_DIM
A_KV = A_KV_HEADS * HEAD_DIM
B_Q = B_HEADS * HEAD_DIM
B_KV = B_KV_HEADS * HEAD_DIM
ATT_IN = A_Q + 2 * A_KV + B_Q + 2 * B_KV
ATT_OUT_IN = A_Q + B_Q

kernel_name = "hybrid_window_grid_attn_sgu_encoder"


def _rmsnorm(x, g):
    xf = x.astype(jnp.float32)
    y = xf * lax.rsqrt(jnp.mean(xf * xf, axis=-1, keepdims=True) + EPS)
    return (y * g.astype(jnp.float32)).astype(x.dtype)


def _layernorm(x, g, b):
    xf = x.astype(jnp.float32)
    mu = jnp.mean(xf, axis=-1, keepdims=True)
    var = jnp.mean(jnp.square(xf - mu), axis=-1, keepdims=True)
    y = (xf - mu) * lax.rsqrt(var + EPS)
    return (y * g.astype(jnp.float32) + b.astype(jnp.float32)).astype(x.dtype)


def _rope_angles(pos, dim):
    freqs = ROPE_THETA ** (-jnp.arange(0, dim, 2, dtype=jnp.float32) / dim)
    ang = pos.astype(jnp.float32)[:, None] * freqs[None, :]
    return jnp.cos(ang), jnp.sin(ang)


def _apply_rope(x, cos, sin):
    xf = x.astype(jnp.float32)
    half = xf.shape[-1] // 2
    x1, x2 = xf[..., :half], xf[..., half:]
    c, s = cos[None, :, None, :], sin[None, :, None, :]
    return jnp.concatenate([x1 * c - x2 * s, x2 * c + x1 * s], axis=-1).astype(x.dtype)


def _apply_axial_rope(x, cos_r, sin_r, cos_c, sin_c):
    half = x.shape[-1] // 2
    return jnp.concatenate([_apply_rope(x[..., :half], cos_r, sin_r),
                            _apply_rope(x[..., half:], cos_c, sin_c)], axis=-1)


def _window_attention(q, k, v, sink):
    bsz, s_len = q.shape[0], q.shape[1]
    nb = s_len // BLOCK
    g = A_HEADS // A_KV_HEADS
    scale = HEAD_DIM ** -0.5
    qb = q.reshape(bsz, nb, BLOCK, A_KV_HEADS, g, HEAD_DIM).astype(jnp.float32)
    pad = ((0, 0), (BLOCK, BLOCK), (0, 0), (0, 0))
    kp = jnp.pad(k, pad).reshape(bsz, nb + 2, BLOCK, A_KV_HEADS, HEAD_DIM)
    vp = jnp.pad(v, pad).reshape(bsz, nb + 2, BLOCK, A_KV_HEADS, HEAD_DIM)
    kband = jnp.concatenate([kp[:, :-2], kp[:, 1:-1], kp[:, 2:]], axis=2).astype(jnp.float32)
    vband = jnp.concatenate([vp[:, :-2], vp[:, 1:-1], vp[:, 2:]], axis=2).astype(jnp.float32)
    s = jnp.einsum('bnqhgd,bnjhd->bnhgqj', qb, kband) * scale
    qi = jnp.arange(BLOCK)
    kj = jnp.arange(3 * BLOCK)
    rel = kj[None, :] - BLOCK - qi[:, None]
    kpos = jnp.arange(nb)[:, None] * BLOCK - BLOCK + kj[None, :]
    mask = (jnp.abs(rel) <= WINDOW)[None, :, :] & ((kpos >= 0) & (kpos < s_len))[:, None, :]
    s = jnp.where(mask[None, :, None, None, :, :], s, -1e30)
    sink_b = sink.astype(jnp.float32).reshape(A_KV_HEADS, g)[None, None, :, :, None, None]
    m = jnp.maximum(jnp.max(s, axis=-1, keepdims=True), sink_b)
    p = jnp.exp(s - m)
    denom = jnp.sum(p, axis=-1, keepdims=True) + jnp.exp(sink_b - m)
    o = jnp.einsum('bnhgqj,bnjhd->bnqhgd', p / denom, vband)
    return o.reshape(bsz, s_len, A_Q).astype(q.dtype)


def _grid_attention(q, k, v):
    bsz, s_len = q.shape[0], q.shape[1]
    nb = s_len // BLOCK
    g = B_HEADS // B_KV_HEADS
    scale = HEAD_DIM ** -0.5
    qb = q.reshape(bsz, nb, BLOCK, B_KV_HEADS, g, HEAD_DIM).transpose(1, 0, 2, 3, 4, 5)
    kf = k.astype(jnp.float32)
    vf = v.astype(jnp.float32)

    def one_block(qblk):
        s = jnp.einsum('bqhgd,bkhd->bhgqk', qblk.astype(jnp.float32), kf) * scale
        p = jax.nn.softmax(s, axis=-1)
        return jnp.einsum('bhgqk,bkhd->bqhgd', p, vf)

    o = lax.map(one_block, qb)
    return o.transpose(1, 0, 2, 3, 4, 5).reshape(bsz, s_len, B_Q).astype(q.dtype)


def _attention_layer(x, norm_g, w_in, sink, qn_g, kn_g, w_out,
                     cos1, sin1, cos_r, sin_r, cos_c, sin_c):
    bsz, s_len = x.shape[0], x.shape[1]
    h = _rmsnorm(x, norm_g)
    proj = h @ w_in
    offs = [A_Q, A_Q + A_KV, A_Q + 2 * A_KV, A_Q + 2 * A_KV + B_Q, A_Q + 2 * A_KV + B_Q + B_KV]
    qa, ka, va, qb, kb, vb = jnp.split(proj, offs, axis=-1)
    qa = qa.reshape(bsz, s_len, A_HEADS, HEAD_DIM)
    ka = ka.reshape(bsz, s_len, A_KV_HEADS, HEAD_DIM)
    va = va.reshape(bsz, s_len, A_KV_HEADS, HEAD_DIM)
    qb = qb.reshape(bsz, s_len, B_HEADS, HEAD_DIM)
    kb = kb.reshape(bsz, s_len, B_KV_HEADS, HEAD_DIM)
    vb = vb.reshape(bsz, s_len, B_KV_HEADS, HEAD_DIM)
    qa = _apply_rope(qa, cos1, sin1)
    ka = _apply_rope(ka, cos1, sin1)
    oa = _window_attention(qa, ka, va, sink)
    qb = _apply_axial_rope(_rmsnorm(qb, qn_g), cos_r, sin_r, cos_c, sin_c)
    kb = _apply_axial_rope(_rmsnorm(kb, kn_g), cos_r, sin_r, cos_c, sin_c)
    ob = _grid_attention(qb, kb, vb)
    return x + jnp.concatenate([oa, ob], axis=-1) @ w_out


def _sgu_layer(x, norm_g, w_in, ln_g, ln_b, w_s, b_s, w_out):
    bsz, s_len = x.shape[0], x.shape[1]
    nc = s_len // SGU_CHUNK
    dg = SGU_WIDTH // SGU_GROUPS
    h = _rmsnorm(x, norm_g)
    z = jax.nn.gelu(h @ w_in)
    u, v = jnp.split(z, 2, axis=-1)
    v = _layernorm(v, ln_g, ln_b)
    vb = v.reshape(bsz, nc, SGU_CHUNK, SGU_GROUPS, dg)
    mixed = jnp.einsum('gpq,bnqgd->bnpgd', w_s, vb) + b_s.T[None, None, :, :, None]
    y = u * mixed.reshape(bsz, s_len, SGU_WIDTH)
    return x + y @ w_out


def _mlp(x, norm_g, w1, w2):
    h = _rmsnorm(x, norm_g)
    return x + jnp.square(jax.nn.relu(h @ w1)) @ w2


def setup_inputs(seed: int = 0) -> dict:
    key = jax.random.key(seed)
    ks = jax.random.split(key, 20)
    f32 = jnp.float32
    nrm = lambda k, shape, s: jax.random.normal(k, shape, f32) * s
    return {
        "x": jax.random.normal(ks[0], (BATCH, SEQ, D_MODEL), f32),
        "att_norm": 1.0 + nrm(ks[1], (N_ATT_LAYERS, D_MODEL), 0.02),
        "att_w_in": nrm(ks[2], (N_ATT_LAYERS, D_MODEL, ATT_IN), D_MODEL ** -0.5),
        "att_sink": nrm(ks[3], (N_ATT_LAYERS, A_HEADS), 0.5),
        "att_qnorm": 1.0 + nrm(ks[4], (N_ATT_LAYERS, HEAD_DIM), 0.02),
        "att_knorm": 1.0 + nrm(ks[5], (N_ATT_LAYERS, HEAD_DIM), 0.02),
        "att_w_out": nrm(ks[6], (N_ATT_LAYERS, ATT_OUT_IN, D_MODEL), ATT_OUT_IN ** -0.5),
        "sgu_norm": 1.0 + nrm(ks[7], (N_SGU_LAYERS, D_MODEL), 0.02),
        "sgu_w_in": nrm(ks[8], (N_SGU_LAYERS, D_MODEL, 2 * SGU_WIDTH), D_MODEL ** -0.5),
        "sgu_ln_g": 1.0 + nrm(ks[9], (N_SGU_LAYERS, SGU_WIDTH), 0.02),
        "sgu_ln_b": nrm(ks[10], (N_SGU_LAYERS, SGU_WIDTH), 0.02),
        "sgu_w_s": nrm(ks[11], (N_SGU_LAYERS, SGU_GROUPS, SGU_CHUNK, SGU_CHUNK), SGU_CHUNK ** -0.5),
        "sgu_b_s": 1.0 + nrm(ks[12], (N_SGU_LAYERS, SGU_GROUPS, SGU_CHUNK), 0.1),
        "sgu_w_out": nrm(ks[13], (N_SGU_LAYERS, SGU_WIDTH, D_MODEL), SGU_WIDTH ** -0.5),
        "mlp_norm": 1.0 + nrm(ks[14], (DEPTH, D_MODEL), 0.02),
        "mlp_w1": nrm(ks[15], (DEPTH, D_MODEL, D_FF), D_MODEL ** -0.5),
        "mlp_w2": nrm(ks[16], (DEPTH, D_FF, D_MODEL), D_FF ** -0.5),
        "final_norm": 1.0 + nrm(ks[17], (D_MODEL,), 0.02),
    }


def reference(x, att_norm, att_w_in, att_sink, att_qnorm, att_knorm, att_w_out,
              sgu_norm, sgu_w_in, sgu_ln_g, sgu_ln_b, sgu_w_s, sgu_b_s, sgu_w_out,
              mlp_norm, mlp_w1, mlp_w2, final_norm):
    s_len = x.shape[1]
    pos = jnp.arange(s_len)
    rows = s_len // GRID_W
    row_idx = jnp.repeat(jnp.arange(rows), GRID_W)
    col_idx = jnp.tile(jnp.arange(GRID_W), rows)
    cos1, sin1 = _rope_angles(pos, HEAD_DIM)
    cos_r, sin_r = _rope_angles(row_idx, HEAD_DIM // 2)
    cos_c, sin_c = _rope_angles(col_idx, HEAD_DIM // 2)
    h = x
    for layer in range(DEPTH):
        i = layer // 2
        if layer % 2 == 0:
            h = _attention_layer(h, att_norm[i], att_w_in[i], att_sink[i], att_qnorm[i],
                                 att_knorm[i], att_w_out[i],
                                 cos1, sin1, cos_r, sin_r, cos_c, sin_c)
        else:
            h = _sgu_layer(h, sgu_norm[i], sgu_w_in[i], sgu_ln_g[i], sgu_ln_b[i],
                           sgu_w_s[i], sgu_b_s[i], sgu_w_out[i])
        h = _mlp(h, mlp_norm[layer], mlp_w1[layer], mlp_w2[layer])
    return _rmsnorm(h, final_norm)
```

```python
import functools

import numpy as np
import jax
import jax.numpy as jnp
from jax import lax
from jax.experimental import pallas as pl
from jax.experimental.pallas import tpu as pltpu

F32 = jnp.float32
BF16 = jnp.bfloat16

HEAD_DIM = 64
N_PAIRS = 4
PAIR_W = 2 * HEAD_DIM
WINDOW = 128
BLOCK = 128
ROPE_THETA = 10000.0
GRID_W = 64
SGU_GROUPS = 8
SGU_CHUNK = 128
EPS = 1e-6
NEG_BIG = -1e30
Q_SCALE = HEAD_DIM ** -0.5

VMEM_LIMIT_BYTES = 56 * 1024 * 1024

TOKEN_TILE = 512
WIN_TILE = 512
GRID_TQ = 256
GRID_TK = 512
ONES_ROWS = 16
VT_ROWS = HEAD_DIM + ONES_ROWS
FF_CHUNK = 1024


def _params(sem):
    return pltpu.CompilerParams(dimension_semantics=sem, vmem_limit_bytes=VMEM_LIMIT_BYTES)


def _resident(shape):
    nd = len(shape)
    return pl.BlockSpec(shape, lambda *_: (0,) * nd, pipeline_mode=pl.Buffered(1))


def _rms(x, g):
    ms = jnp.mean(x * x, axis=-1, keepdims=True)
    return x * lax.rsqrt(ms + EPS) * g


def _attn_in_kernel(x_ref, g_ref, w_ref, ca_ref, sa_ref, cb_ref, sb_ref, qn_ref, kn_ref, seg_ref,
                    qa_ref, ka_ref, va_ref, qbt_ref, kb_ref, vbt_ref):
    tm = x_ref.shape[0]
    hn = _rms(x_ref[...], g_ref[...]).astype(BF16)
    proj = jnp.dot(hn, w_ref[...], preferred_element_type=F32)

    lane = lax.broadcasted_iota(jnp.int32, (tm, PAIR_W), 1)
    first32 = (lane % HEAD_DIM) < (HEAD_DIM // 2)
    first16 = (lane % (HEAD_DIM // 2)) < (HEAD_DIM // 4)
    ca, sa, cb, sb = ca_ref[...], sa_ref[...], cb_ref[...], sb_ref[...]
    seg = seg_ref[...]

    def rope_a(t):
        partner = jnp.where(first32, pltpu.roll(t, PAIR_W - 32, 1), pltpu.roll(t, 32, 1))
        return t * ca + partner * sa

    def rope_b(t):
        partner = jnp.where(first16, pltpu.roll(t, PAIR_W - 16, 1), pltpu.roll(t, 16, 1))
        return t * cb + partner * sb

    def headnorm(t, gain):
        t2 = t * t
        hi = t2.astype(BF16)
        lo = (t2 - hi.astype(F32)).astype(BF16)
        ss = (jnp.dot(hi, seg, preferred_element_type=F32)
              + jnp.dot(lo, seg, preferred_element_type=F32))
        return t * lax.rsqrt(ss * (1.0 / HEAD_DIM) + EPS) * gain

    def blk(i):
        return proj[:, i * PAIR_W:(i + 1) * PAIR_W]

    for p in range(N_PAIRS):
        qa_ref[:, p * PAIR_W:(p + 1) * PAIR_W] = (rope_a(blk(p)) * Q_SCALE).astype(BF16)
    ka_ref[...] = rope_a(blk(4)).astype(BF16)
    va_ref[...] = blk(5).astype(BF16)
    qn, kn = qn_ref[...], kn_ref[...]
    for p in range(N_PAIRS):
        t = rope_b(headnorm(blk(6 + p), qn)) * Q_SCALE
        qbt_ref[0, p * PAIR_W:(p + 1) * PAIR_W, :] = t.T.astype(BF16)
    kb_ref[...] = rope_b(headnorm(blk(10), kn)).astype(BF16)
    vt = blk(11).T
    ones = jnp.ones((ONES_ROWS, tm), F32)
    vbt_ref[0, 0] = jnp.concatenate(
        [vt[:HEAD_DIM], ones, vt[HEAD_DIM:], ones], axis=0).astype(BF16)


def _attn_in(x2d, g, w, tabs, qn, kn, seg, bsz, s_len):
    t_rows, d = x2d.shape
    tm = TOKEN_TILE
    n_s = s_len // tm
    n_cols = w.shape[1]
    row_blk = lambda c: pl.BlockSpec((tm, c), lambda i: (i, 0))
    tab_blk = pl.BlockSpec((tm, PAIR_W), lambda i: (i % n_s, 0))
    vec = lambda c: pl.BlockSpec((1, c), lambda i: (0, 0))
    out_shape = (
        jax.ShapeDtypeStruct((t_rows, N_PAIRS * PAIR_W), BF16),
        jax.ShapeDtypeStruct((t_rows, PAIR_W), BF16),
        jax.ShapeDtypeStruct((t_rows, PAIR_W), BF16),
        jax.ShapeDtypeStruct((bsz, N_PAIRS * PAIR_W, s_len), BF16),
        jax.ShapeDtypeStruct((t_rows, PAIR_W), BF16),
        jax.ShapeDtypeStruct((bsz, n_s, 2 * VT_ROWS, tm), BF16),
    )
    out_specs = (
        row_blk(N_PAIRS * PAIR_W), row_blk(PAIR_W), row_blk(PAIR_W),
        pl.BlockSpec((1, N_PAIRS * PAIR_W, tm), lambda i: (i // n_s, 0, i % n_s)),
        row_blk(PAIR_W),
        pl.BlockSpec((1, 1, 2 * VT_ROWS, tm), lambda i: (i // n_s, i % n_s, 0, 0)),
    )
    return pl.pallas_call(
        _attn_in_kernel,
        grid=(t_rows // tm,),
        in_specs=[row_blk(d), vec(d), _resident((d, n_cols)),
                  tab_blk, tab_blk, tab_blk, tab_blk,
                  vec(PAIR_W), vec(PAIR_W), _resident((PAIR_W, PAIR_W))],
        out_specs=out_specs,
        out_shape=out_shape,
        compiler_params=_params(("parallel",)),
        name="attn_in",
    )(x2d, g, w, *tabs, qn, kn, seg)


def _window_kernel(sink_ref, q_ref, kp_ref, kc_ref, kn_ref, vp_ref, vc_ref, vn_ref, o_ref, *, s_len):
    tq = q_ref.shape[1]
    nblk = tq // BLOCK
    i = pl.program_id(1)
    kfull = jnp.concatenate([kp_ref[0], kc_ref[0], kn_ref[0]], axis=0)
    vfull = jnp.concatenate([vp_ref[0], vc_ref[0], vn_ref[0]], axis=0)
    lane_k = lax.broadcasted_iota(jnp.int32, kfull.shape, 1)
    zero = jnp.zeros_like(kfull)
    k_half = (jnp.where(lane_k < HEAD_DIM, kfull, zero), jnp.where(lane_k >= HEAD_DIM, kfull, zero))

    qi = lax.broadcasted_iota(jnp.int32, (BLOCK, 3 * BLOCK), 0)
    kj = lax.broadcasted_iota(jnp.int32, (BLOCK, 3 * BLOCK), 1)
    band = jnp.abs(kj - BLOCK - qi) <= WINDOW
    lane_o = lax.broadcasted_iota(jnp.int32, (BLOCK, PAIR_W), 1)

    for j in range(nblk):
        kpos = (i * nblk + j - 1) * BLOCK + kj
        mask = band & (kpos >= 0) & (kpos < s_len)
        vband = vfull[j * BLOCK:(j + 3) * BLOCK]
        for p in range(N_PAIRS):
            q = q_ref[0, j * BLOCK:(j + 1) * BLOCK, p * PAIR_W:(p + 1) * PAIR_W]
            halves = []
            for half in range(2):
                kband = k_half[half][j * BLOCK:(j + 3) * BLOCK]
                s = lax.dot_general(q, kband, (((1,), (1,)), ((), ())),
                                    preferred_element_type=F32)
                s = jnp.where(mask, s, NEG_BIG)
                sink = sink_ref[half * N_PAIRS + p]
                m = jnp.maximum(jnp.max(s, axis=-1, keepdims=True), sink)
                e = jnp.exp(s - m)
                denom = jnp.sum(e, axis=-1, keepdims=True) + jnp.exp(sink - m)
                pv = jnp.dot(e.astype(BF16), vband, preferred_element_type=F32)
                halves.append(pv / denom)
            o_ref[0, j * BLOCK:(j + 1) * BLOCK, p * PAIR_W:(p + 1) * PAIR_W] = jnp.where(
                lane_o < HEAD_DIM, halves[0], halves[1]).astype(BF16)


def _window_attn(sink, qa, ka, va):
    bsz, s_len, qw = qa.shape
    tq = WIN_TILE
    r = tq // BLOCK
    nb = s_len // BLOCK
    cur = pl.BlockSpec((1, tq, PAIR_W), lambda b, i: (b, i, 0))
    prev = pl.BlockSpec((1, BLOCK, PAIR_W), lambda b, i: (b, jnp.maximum(i * r - 1, 0), 0))
    nxt = pl.BlockSpec((1, BLOCK, PAIR_W), lambda b, i: (b, jnp.minimum(i * r + r, nb - 1), 0))
    return pl.pallas_call(
        functools.partial(_window_kernel, s_len=s_len),
        grid=(bsz, s_len // tq),
        in_specs=[pl.BlockSpec(memory_space=pltpu.SMEM),
                  pl.BlockSpec((1, tq, qw), lambda b, i: (b, i, 0)),
                  prev, cur, nxt, prev, cur, nxt],
        out_specs=pl.BlockSpec((1, tq, qw), lambda b, i: (b, i, 0)),
        out_shape=jax.ShapeDtypeStruct((bsz, s_len, qw), BF16),
        compiler_params=_params(("parallel", "parallel")),
        name="window_attn",
    )(sink, qa, ka, ka, ka, va, va, va)


def _grid_kernel(qt_ref, k_ref, vt_ref, o_ref, qx_ref, m_ref, acc_ref):
    tq = qt_ref.shape[2]
    n_kc, _, tk = vt_ref.shape[1:]
    n_heads = 2 * N_PAIRS
    row = lax.broadcasted_iota(jnp.int32, (PAIR_W, tq), 0)
    for p in range(N_PAIRS):
        blk = qt_ref[0, p * PAIR_W:(p + 1) * PAIR_W, :]
        zero = jnp.zeros_like(blk)
        qx_ref[2 * p] = jnp.where(row < HEAD_DIM, blk, zero)
        qx_ref[2 * p + 1] = jnp.where(row >= HEAD_DIM, blk, zero)
    m_ref[...] = jnp.full(m_ref.shape, NEG_BIG, F32)
    acc_ref[...] = jnp.zeros(acc_ref.shape, F32)

    def body(kk, carry):
        kblk = k_ref[0, pl.ds(pl.multiple_of(kk * tk, tk), tk), :]
        vblk = vt_ref[0, kk]
        for h in range(n_heads):
            half = h % 2
            s_t = jnp.dot(kblk, qx_ref[h], preferred_element_type=F32)
            m_old = m_ref[h]
            m_new = jnp.maximum(m_old, jnp.max(s_t, axis=0, keepdims=True))
            alpha = jnp.exp(m_old - m_new)
            p_t = jnp.exp(s_t - m_new).astype(BF16)
            v_t = vblk[half * VT_ROWS:(half + 1) * VT_ROWS]
            acc_ref[h] = alpha * acc_ref[h] + jnp.dot(v_t, p_t, preferred_element_type=F32)
            m_ref[h] = m_new
        return carry

    lax.fori_loop(0, n_kc, body, 0)

    outs = []
    for h in range(n_heads):
        a = acc_ref[h]
        outs.append(a[:HEAD_DIM] / a[HEAD_DIM:HEAD_DIM + 1])
    o_ref[0] = jnp.concatenate(outs, axis=0).T.astype(BF16)


def _grid_attn(qbt, kb, vbt):
    bsz, qw, s_len = qbt.shape
    n_kc, vrows, tk = vbt.shape[1:]
    tq = GRID_TQ
    n_heads = 2 * N_PAIRS
    return pl.pallas_call(
        _grid_kernel,
        grid=(bsz, s_len // tq),
        in_specs=[pl.BlockSpec((1, qw, tq), lambda b, i: (b, 0, i)),
                  pl.BlockSpec((1, s_len, PAIR_W), lambda b, i: (b, 0, 0)),
                  pl.BlockSpec((1, n_kc, vrows, tk), lambda b, i: (b, 0, 0, 0))],
        out_specs=pl.BlockSpec((1, tq, qw), lambda b, i: (b, i, 0)),
        out_shape=jax.ShapeDtypeStruct((bsz, s_len, qw), BF16),
        scratch_shapes=[pltpu.VMEM((n_heads, PAIR_W, tq), BF16),
                        pltpu.VMEM((n_heads, 1, tq), F32),
                        pltpu.VMEM((n_heads, VT_ROWS, tq), F32)],
        compiler_params=_params(("parallel", "arbitrary")),
        name="grid_attn",
    )(qbt, kb, vbt)


def _proj_mlp_kernel(*refs, n_y, final):
    x_ref = refs[0]
    y_refs = refs[1:1 + n_y]
    wo_ref, g_ref, w1_ref, w2_ref, gf_ref, o_ref = refs[1 + n_y:]
    y = y_refs[0][...] if n_y == 1 else jnp.concatenate([r[...] for r in y_refs], axis=1)
    x1 = x_ref[...] + jnp.dot(y, wo_ref[...], preferred_element_type=F32)
    hn = _rms(x1, g_ref[...]).astype(BF16)
    acc = x1
    d_ff = w1_ref.shape[1]
    for c in range(d_ff // FF_CHUNK):
        h = jnp.dot(hn, w1_ref[:, c * FF_CHUNK:(c + 1) * FF_CHUNK], preferred_element_type=F32)
        h = jnp.maximum(h, 0.0)
        acc = acc + jnp.dot((h * h).astype(BF16), w2_ref[c * FF_CHUNK:(c + 1) * FF_CHUNK, :],
                            preferred_element_type=F32)
    if final:
        acc = _rms(acc, gf_ref[...])
    o_ref[...] = acc


def _proj_mlp(x2d, ys, wo, g, w1, w2, gf, final):
    t_rows, d = x2d.shape
    tm = TOKEN_TILE
    row_blk = lambda c: pl.BlockSpec((tm, c), lambda i: (i, 0))
    vec = pl.BlockSpec((1, d), lambda i: (0, 0))
    in_specs = ([row_blk(d)] + [row_blk(y.shape[1]) for y in ys]
                + [_resident(wo.shape), vec, _resident(w1.shape), _resident(w2.shape), vec])
    return pl.pallas_call(
        functools.partial(_proj_mlp_kernel, n_y=len(ys), final=final),
        grid=(t_rows // tm,),
        in_specs=in_specs,
        out_specs=row_blk(d),
        out_shape=jax.ShapeDtypeStruct((t_rows, d), F32),
        compiler_params=_params(("parallel",)),
        name="proj_mlp",
    )(x2d, *ys, wo, g, w1, w2, gf)


def _sgu_kernel(x_ref, g_ref, w_ref, lg_ref, lb_ref, ws_ref, bst_ref, y_ref):
    tm = x_ref.shape[0]
    width = y_ref.shape[1]
    dg = width // SGU_GROUPS
    n_chunks = tm // SGU_CHUNK
    hn = _rms(x_ref[...], g_ref[...]).astype(BF16)
    u = jax.nn.gelu(jnp.dot(hn, w_ref[:, :width], preferred_element_type=F32), approximate=True)
    v = jax.nn.gelu(jnp.dot(hn, w_ref[:, width:], preferred_element_type=F32), approximate=True)
    mu = jnp.mean(v, axis=-1, keepdims=True)
    vc = v - mu
    var = jnp.mean(vc * vc, axis=-1, keepdims=True)
    v = (vc * lax.rsqrt(var + EPS) * lg_ref[...] + lb_ref[...]).astype(BF16)
    bst = bst_ref[...]
    for gi in range(SGU_GROUPS):
        cols = slice(gi * dg, (gi + 1) * dg)
        vg = jnp.concatenate(
            [v[c * SGU_CHUNK:(c + 1) * SGU_CHUNK, cols] for c in range(n_chunks)], axis=1)
        mixed = jnp.dot(ws_ref[gi], vg, preferred_element_type=F32) + bst[:, gi:gi + 1]
        for c in range(n_chunks):
            rows = slice(c * SGU_CHUNK, (c + 1) * SGU_CHUNK)
            y_ref[rows, cols] = (u[rows, cols] * mixed[:, c * dg:(c + 1) * dg]).astype(BF16)


def _sgu(x2d, g, w_in, ln_g, ln_b, w_s, b_st):
    t_rows, d = x2d.shape
    width = w_in.shape[1] // 2
    tm = TOKEN_TILE
    vec = lambda c: pl.BlockSpec((1, c), lambda i: (0, 0))
    return pl.pallas_call(
        _sgu_kernel,
        grid=(t_rows // tm,),
        in_specs=[pl.BlockSpec((tm, d), lambda i: (i, 0)), vec(d), _resident(w_in.shape),
                  vec(width), vec(width), _resident(w_s.shape), _resident(b_st.shape)],
        out_specs=pl.BlockSpec((tm, width), lambda i: (i, 0)),
        out_shape=jax.ShapeDtypeStruct((t_rows, width), BF16),
        compiler_params=_params(("parallel",)),
        name="sgu",
    )(x2d, g, w_in, ln_g, ln_b, w_s, b_st)


def _pair_order(n_heads):
    half = n_heads // 2
    return [h for p in range(half) for h in (p, half + p)]


def _rope_tables(s_len):
    def angles(pos, dim):
        freqs = ROPE_THETA ** (-jnp.arange(0, dim, 2, dtype=F32) / dim)
        ang = pos.astype(F32)[:, None] * freqs[None, :]
        return jnp.cos(ang), jnp.sin(ang)

    pos = jnp.arange(s_len)
    cos1, sin1 = angles(pos, HEAD_DIM)
    cos_r, sin_r = angles(pos // GRID_W, HEAD_DIM // 2)
    cos_c, sin_c = angles(pos % GRID_W, HEAD_DIM // 2)
    ca = jnp.tile(cos1, (1, 4))
    sa = jnp.tile(jnp.concatenate([-sin1, sin1], axis=1), (1, 2))
    cb = jnp.tile(jnp.concatenate([cos_r, cos_r, cos_c, cos_c], axis=1), (1, 2))
    sb = jnp.tile(jnp.concatenate([-sin_r, sin_r, -sin_c, sin_c], axis=1), (1, 2))
    return ca, sa, cb, sb


def kernel(x, att_norm, att_w_in, att_sink, att_qnorm, att_knorm, att_w_out, sgu_norm, sgu_w_in,
           sgu_ln_g, sgu_ln_b, sgu_w_s, sgu_b_s, sgu_w_out, mlp_norm, mlp_w1, mlp_w2, final_norm):
    bsz, s_len, d = x.shape
    depth = mlp_w1.shape[0]
    n_q = 2 * N_PAIRS
    qw = n_q * HEAD_DIM
    kvw = 2 * HEAD_DIM
    assert s_len % max(TOKEN_TILE, WIN_TILE, GRID_TQ, GRID_TK) == 0 and GRID_TK == TOKEN_TILE

    order = np.asarray(_pair_order(n_q))
    q_cols = (order[:, None] * HEAD_DIM + np.arange(HEAD_DIM)[None, :]).reshape(-1)
    offs_b = qw + 2 * kvw
    in_cols = np.concatenate([q_cols, np.arange(qw, offs_b), offs_b + q_cols,
                              np.arange(offs_b + qw, offs_b + qw + 2 * kvw)])
    out_rows = np.concatenate([q_cols, qw + q_cols])

    tabs = _rope_tables(s_len)
    lane_head = np.arange(PAIR_W) // HEAD_DIM
    seg = jnp.asarray(lane_head[:, None] == lane_head[None, :], BF16)
    row = lambda v: v.reshape(1, -1).astype(F32)

    h = x.reshape(bsz * s_len, d)
    for layer in range(depth):
        i = layer // 2
        if layer % 2 == 0:
            w_in = att_w_in[i][:, in_cols].astype(BF16)
            w_out = att_w_out[i][out_rows, :].astype(BF16)
            qn = row(jnp.tile(att_qnorm[i], 2))
            kn = row(jnp.tile(att_knorm[i], 2))
            qa, ka, va, qbt, kb, vbt = _attn_in(h, row(att_norm[i]), w_in, tabs, qn, kn, seg,
                                                bsz, s_len)
            r3 = lambda a: a.reshape(bsz, s_len, a.shape[-1])
            oa = _window_attn(att_sink[i].astype(F32), r3(qa), r3(ka), r3(va))
            ob = _grid_attn(qbt, r3(kb), vbt)
            ys = [oa.reshape(bsz * s_len, qw), ob.reshape(bsz * s_len, qw)]
            wo = w_out
        else:
            y = _sgu(h, row(sgu_norm[i]), sgu_w_in[i].astype(BF16), row(sgu_ln_g[i]),
                     row(sgu_ln_b[i]), sgu_w_s[i].astype(BF16), sgu_b_s[i].T.astype(F32))
            ys = [y]
            wo = sgu_w_out[i].astype(BF16)
        h = _proj_mlp(h, ys, wo, row(mlp_norm[layer]), mlp_w1[layer].astype(BF16),
                      mlp_w2[layer].astype(BF16), row(final_norm), final=(layer == depth - 1))
    return h.reshape(bsz, s_len, d)
```

```python
import functools

import numpy as np
import jax
import jax.numpy as jnp
from jax import lax
from jax.experimental import pallas as pl
from jax.experimental.pallas import tpu as pltpu

F32 = jnp.float32
BF16 = jnp.bfloat16

HEAD_DIM = 64
N_PAIRS = 4
PAIR_W = 2 * HEAD_DIM
WINDOW = 128
BLOCK = 128
ROPE_THETA = 10000.0
GRID_W = 64
SGU_GROUPS = 8
SGU_CHUNK = 128
EPS = 1e-6
NEG_BIG = -1e30
LOG2E = 1.4426950408889634
Q_SCALE = HEAD_DIM ** -0.5 * LOG2E

VMEM_LIMIT_BYTES = 56 * 1024 * 1024

TOKEN_TILE = 512
ATT_TQ = 256
WIN_KEYS = ATT_TQ + 2 * BLOCK
WIN_STEP_TILES = 8
GRID_TK = 1024
PIPE_DEPTH = 3
N_SLOTS = PIPE_DEPTH + 1
ONES_ROWS = 16
VT_ROWS = HEAD_DIM + ONES_ROWS
FF_CHUNK = 1024


def _params(sem):
    return pltpu.CompilerParams(dimension_semantics=sem, vmem_limit_bytes=VMEM_LIMIT_BYTES)


def _resident(shape):
    nd = len(shape)
    return pl.BlockSpec(shape, lambda *_: (0,) * nd, pipeline_mode=pl.Buffered(1))


def _rms(x, g):
    ms = jnp.mean(x * x, axis=-1, keepdims=True)
    return x * lax.rsqrt(ms + EPS) * g


def _with_ones(vt):
    ones = jnp.ones((ONES_ROWS, vt.shape[1]), F32)
    return jnp.concatenate([vt[:HEAD_DIM], ones, vt[HEAD_DIM:], ones], axis=0).astype(BF16)


def _attn_in_kernel(x_ref, g_ref, w_ref, ca_ref, sa_ref, cb_ref, sb_ref, qn_ref, kn_ref, seg_ref,
                    qat_ref, ka_ref, vat_ref, qbt_ref, kb_ref, vbt_ref):
    tm = x_ref.shape[0]
    hn = _rms(x_ref[...], g_ref[...]).astype(BF16)
    proj = jnp.dot(hn, w_ref[...], preferred_element_type=F32)

    lane = lax.broadcasted_iota(jnp.int32, (tm, PAIR_W), 1)
    first32 = (lane % HEAD_DIM) < (HEAD_DIM // 2)
    first16 = (lane % (HEAD_DIM // 2)) < (HEAD_DIM // 4)
    ca, sa, cb, sb = ca_ref[...], sa_ref[...], cb_ref[...], sb_ref[...]
    seg = seg_ref[...]

    def rope_a(t):
        partner = jnp.where(first32, pltpu.roll(t, PAIR_W - 32, 1), pltpu.roll(t, 32, 1))
        return t * ca + partner * sa

    def rope_b(t):
        partner = jnp.where(first16, pltpu.roll(t, PAIR_W - 16, 1), pltpu.roll(t, 16, 1))
        return t * cb + partner * sb

    def headnorm(t, gain):
        t2 = t * t
        hi = t2.astype(BF16)
        lo = (t2 - hi.astype(F32)).astype(BF16)
        ss = (jnp.dot(hi, seg, preferred_element_type=F32)
              + jnp.dot(lo, seg, preferred_element_type=F32))
        return t * lax.rsqrt(ss * (1.0 / HEAD_DIM) + EPS) * gain

    def blk(i):
        return proj[:, i * PAIR_W:(i + 1) * PAIR_W]

    for p in range(N_PAIRS):
        t = (rope_a(blk(p)) * Q_SCALE).T.astype(BF16)
        for c in range(tm // ATT_TQ):
            qat_ref[0, c, p * PAIR_W:(p + 1) * PAIR_W, :] = t[:, c * ATT_TQ:(c + 1) * ATT_TQ]
    ka_ref[...] = rope_a(blk(4)).astype(BF16)
    vat = _with_ones(blk(5).T)
    for c in range(tm // BLOCK):
        vat_ref[0, c] = vat[:, c * BLOCK:(c + 1) * BLOCK]
    qn, kn = qn_ref[...], kn_ref[...]
    for p in range(N_PAIRS):
        t = rope_b(headnorm(blk(6 + p), qn)) * Q_SCALE
        qbt_ref[0, p * PAIR_W:(p + 1) * PAIR_W, :] = t.T.astype(BF16)
    kb_ref[...] = rope_b(headnorm(blk(10), kn)).astype(BF16)
    vbt_ref[0, 0] = _with_ones(blk(11).T)


def _attn_in(x2d, g, w, tabs, qn, kn, seg, bsz, s_len):
    t_rows, d = x2d.shape
    tm = TOKEN_TILE
    n_s = s_len // tm
    n_cols = w.shape[1]
    qw = N_PAIRS * PAIR_W
    r_q, r_b, r_k = tm // ATT_TQ, tm // BLOCK, GRID_TK // tm
    row_blk = lambda c: pl.BlockSpec((tm, c), lambda i: (i, 0))
    tab_blk = pl.BlockSpec((tm, PAIR_W), lambda i: (i % n_s, 0))
    vec = lambda c: pl.BlockSpec((1, c), lambda i: (0, 0))
    out_shape = (
        jax.ShapeDtypeStruct((bsz, s_len // ATT_TQ, qw, ATT_TQ), BF16),
        jax.ShapeDtypeStruct((t_rows, PAIR_W), BF16),
        jax.ShapeDtypeStruct((bsz, s_len // BLOCK, 2 * VT_ROWS, BLOCK), BF16),
        jax.ShapeDtypeStruct((bsz, qw, s_len), BF16),
        jax.ShapeDtypeStruct((t_rows, PAIR_W), BF16),
        jax.ShapeDtypeStruct((bsz, s_len // GRID_TK, 2 * VT_ROWS, GRID_TK), BF16),
    )
    out_specs = (
        pl.BlockSpec((1, r_q, qw, ATT_TQ), lambda i: (i // n_s, i % n_s, 0, 0)),
        row_blk(PAIR_W),
        pl.BlockSpec((1, r_b, 2 * VT_ROWS, BLOCK), lambda i: (i // n_s, i % n_s, 0, 0)),
        pl.BlockSpec((1, qw, tm), lambda i: (i // n_s, 0, i % n_s)),
        row_blk(PAIR_W),
        pl.BlockSpec((1, 1, 2 * VT_ROWS, tm), lambda i: (i // n_s, (i % n_s) // r_k, 0, (i % n_s) % r_k)),
    )
    return pl.pallas_call(
        _attn_in_kernel,
        grid=(t_rows // tm,),
        in_specs=[row_blk(d), vec(d), _resident((d, n_cols)),
                  tab_blk, tab_blk, tab_blk, tab_blk,
                  vec(PAIR_W), vec(PAIR_W), _resident((PAIR_W, PAIR_W))],
        out_specs=out_specs,
        out_shape=out_shape,
        compiler_params=_params(("parallel",)),
        name="attn_in",
    )(x2d, g, w, *tabs, qn, kn, seg)


def _pipelined(n_outer, n_heads, scores, update):
    def steps(o, wrap):
        for h in range(n_heads):
            nh = h + PIPE_DEPTH
            if nh < n_heads:
                scores(o, nh)
            elif wrap:
                scores(o + 1, nh - n_heads)
            update(o, h)

    for d in range(PIPE_DEPTH):
        scores(0, d)

    def body(o, carry):
        steps(o, True)
        return carry

    lax.fori_loop(0, n_outer - 1, body, 0)
    steps(n_outer - 1, False)


def _masked_q(qt, h):
    row = lax.broadcasted_iota(jnp.int32, qt.shape, 0)
    keep = (row < HEAD_DIM) if h % 2 == 0 else (row >= HEAD_DIM)
    return jnp.where(keep, qt, jnp.zeros_like(qt))


def _window_kernel(sink_ref, qt_ref, k_ref, vt_ref, o_ref, bias_ref, s_ref, m_ref, ot_ref, *, s_len):
    n_tiles, qw, tq = qt_ref.shape[1:]
    n_heads = 2 * N_PAIRS
    n_blk = s_len // BLOCK
    blk_per_tile = tq // BLOCK
    n_kblk = WIN_KEYS // BLOCK
    step = pl.program_id(1)
    kj = lax.broadcasted_iota(jnp.int32, (WIN_KEYS, tq), 0)
    qi = lax.broadcasted_iota(jnp.int32, (WIN_KEYS, tq), 1)
    band = jnp.abs(kj - BLOCK - qi) <= WINDOW

    def first_block(t):
        return (step * n_tiles + t) * blk_per_tile - 1

    def key_blocks(t):
        b0 = first_block(t)
        return [jnp.clip(b0 + c, 0, n_blk - 1) for c in range(n_kblk)]

    def scores(t, h):
        if h == 0:
            kpos = first_block(t) * BLOCK + kj
            ok = band & (kpos >= 0) & (kpos < s_len)
            bias_ref[...] = jnp.where(ok, 0.0, NEG_BIG)
        kband = jnp.concatenate(
            [k_ref[0, pl.ds(pl.multiple_of(b * BLOCK, BLOCK), BLOCK), :] for b in key_blocks(t)], axis=0)
        p = h // 2
        qx = _masked_q(qt_ref[0, t, p * PAIR_W:(p + 1) * PAIR_W, :], h)
        s_t = jnp.dot(kband, qx, preferred_element_type=F32) + bias_ref[...]
        s_ref[h % N_SLOTS] = s_t
        sink = sink_ref[(h % 2) * N_PAIRS + p] * LOG2E
        m_ref[h % N_SLOTS] = jnp.maximum(jnp.max(s_t, axis=0, keepdims=True), sink)

    def update(t, h):
        slot = h % N_SLOTS
        m = m_ref[slot]
        p_t = jnp.exp2(s_ref[slot] - m).astype(BF16)
        rows = slice((h % 2) * VT_ROWS, (h % 2 + 1) * VT_ROWS)
        v_t = jnp.concatenate([vt_ref[0, b, rows, :] for b in key_blocks(t)], axis=1)
        r = jnp.dot(v_t, p_t, preferred_element_type=F32)
        sink = sink_ref[(h % 2) * N_PAIRS + h // 2] * LOG2E
        denom = r[HEAD_DIM:HEAD_DIM + 1] + jnp.exp2(sink - m)
        ot_ref[h * HEAD_DIM:(h + 1) * HEAD_DIM, :] = r[:HEAD_DIM] / denom
        if h == n_heads - 1:
            o_ref[0, pl.ds(pl.multiple_of(t * tq, tq), tq), :] = ot_ref[...].T.astype(BF16)

    _pipelined(n_tiles, n_heads, scores, update)


def _window_attn(sink, qat, ka, vat):
    bsz, n_qt, qw, tq = qat.shape
    s_len = ka.shape[1]
    n_t = WIN_STEP_TILES
    return pl.pallas_call(
        functools.partial(_window_kernel, s_len=s_len),
        grid=(bsz, n_qt // n_t),
        in_specs=[pl.BlockSpec(memory_space=pltpu.SMEM),
                  pl.BlockSpec((1, n_t, qw, tq), lambda b, i: (b, i, 0, 0)),
                  pl.BlockSpec((1, s_len, PAIR_W), lambda b, i: (b, 0, 0)),
                  pl.BlockSpec((1,) + vat.shape[1:], lambda b, i: (b, 0, 0, 0))],
        out_specs=pl.BlockSpec((1, n_t * tq, qw), lambda b, i: (b, i, 0)),
        out_shape=jax.ShapeDtypeStruct((bsz, s_len, qw), BF16),
        scratch_shapes=[pltpu.VMEM((WIN_KEYS, tq), F32),
                        pltpu.VMEM((N_SLOTS, WIN_KEYS, tq), F32),
                        pltpu.VMEM((N_SLOTS, 1, tq), F32),
                        pltpu.VMEM((qw, tq), F32)],
        compiler_params=_params(("parallel", "arbitrary")),
        name="window_attn",
    )(sink, qat, ka, vat)


def _grid_kernel(qt_ref, k_ref, vt_ref, o_ref, qx_ref, m_ref, acc_ref, s_ref, al_ref):
    tq = qt_ref.shape[2]
    n_kc, _, tk = vt_ref.shape[1:]
    n_heads = 2 * N_PAIRS
    for h in range(n_heads):
        p = h // 2
        qx_ref[h] = _masked_q(qt_ref[0, p * PAIR_W:(p + 1) * PAIR_W, :], h)
    m_ref[...] = jnp.full(m_ref.shape, NEG_BIG, F32)
    acc_ref[...] = jnp.zeros(acc_ref.shape, F32)

    def scores(kk, h):
        slot = h % N_SLOTS
        kblk = k_ref[0, pl.ds(pl.multiple_of(kk * tk, tk), tk), :]
        s_t = jnp.dot(kblk, qx_ref[h], preferred_element_type=F32)
        s_ref[slot] = s_t
        m_old = m_ref[h]
        m_new = jnp.maximum(m_old, jnp.max(s_t, axis=0, keepdims=True))
        al_ref[slot] = jnp.exp2(m_old - m_new)
        m_ref[h] = m_new

    def update(kk, h):
        slot = h % N_SLOTS
        p_t = jnp.exp2(s_ref[slot] - m_ref[h]).astype(BF16)
        v_t = vt_ref[0, kk, (h % 2) * VT_ROWS:(h % 2 + 1) * VT_ROWS, :]
        acc_ref[h] = al_ref[slot] * acc_ref[h] + jnp.dot(v_t, p_t, preferred_element_type=F32)

    _pipelined(n_kc, n_heads, scores, update)

    outs = []
    for h in range(n_heads):
        a = acc_ref[h]
        outs.append(a[:HEAD_DIM] / a[HEAD_DIM:HEAD_DIM + 1])
    o_ref[0] = jnp.concatenate(outs, axis=0).T.astype(BF16)


def _grid_attn(qbt, kb, vbt):
    bsz, qw, s_len = qbt.shape
    n_kc, vrows, tk = vbt.shape[1:]
    tq = ATT_TQ
    n_heads = 2 * N_PAIRS
    return pl.pallas_call(
        _grid_kernel,
        grid=(bsz, s_len // tq),
        in_specs=[pl.BlockSpec((1, qw, tq), lambda b, i: (b, 0, i)),
                  pl.BlockSpec((1, s_len, PAIR_W), lambda b, i: (b, 0, 0)),
                  pl.BlockSpec((1, n_kc, vrows, tk), lambda b, i: (b, 0, 0, 0))],
        out_specs=pl.BlockSpec((1, tq, qw), lambda b, i: (b, i, 0)),
        out_shape=jax.ShapeDtypeStruct((bsz, s_len, qw), BF16),
        scratch_shapes=[pltpu.VMEM((n_heads, PAIR_W, tq), BF16),
                        pltpu.VMEM((n_heads, 1, tq), F32),
                        pltpu.VMEM((n_heads, VT_ROWS, tq), F32),
                        pltpu.VMEM((N_SLOTS, tk, tq), F32),
                        pltpu.VMEM((N_SLOTS, 1, tq), F32)],
        compiler_params=_params(("parallel", "arbitrary")),
        name="grid_attn",
    )(qbt, kb, vbt)


def _proj_mlp_kernel(*refs, n_y, final):
    x_ref = refs[0]
    y_refs = refs[1:1 + n_y]
    wo_ref, g_ref, w1_ref, w2_ref, gf_ref, o_ref = refs[1 + n_y:]
    y = y_refs[0][...] if n_y == 1 else jnp.concatenate([r[...] for r in y_refs], axis=1)
    x1 = x_ref[...] + jnp.dot(y, wo_ref[...], preferred_element_type=F32)
    hn = _rms(x1, g_ref[...]).astype(BF16)
    acc = x1
    d_ff = w1_ref.shape[1]
    for c in range(d_ff // FF_CHUNK):
        h = jnp.dot(hn, w1_ref[:, c * FF_CHUNK:(c + 1) * FF_CHUNK], preferred_element_type=F32)
        h = jnp.maximum(h, 0.0)
        acc = acc + jnp.dot((h * h).astype(BF16), w2_ref[c * FF_CHUNK:(c + 1) * FF_CHUNK, :],
                            preferred_element_type=F32)
    if final:
        acc = _rms(acc, gf_ref[...])
    o_ref[...] = acc


def _proj_mlp(x2d, ys, wo, g, w1, w2, gf, final):
    t_rows, d = x2d.shape
    tm = TOKEN_TILE
    row_blk = lambda c: pl.BlockSpec((tm, c), lambda i: (i, 0))
    vec = pl.BlockSpec((1, d), lambda i: (0, 0))
    in_specs = ([row_blk(d)] + [row_blk(y.shape[1]) for y in ys]
                + [_resident(wo.shape), vec, _resident(w1.shape), _resident(w2.shape), vec])
    return pl.pallas_call(
        functools.partial(_proj_mlp_kernel, n_y=len(ys), final=final),
        grid=(t_rows // tm,),
        in_specs=in_specs,
        out_specs=row_blk(d),
        out_shape=jax.ShapeDtypeStruct((t_rows, d), F32),
        compiler_params=_params(("parallel",)),
        name="proj_mlp",
    )(x2d, *ys, wo, g, w1, w2, gf)


def _sgu_kernel(x_ref, g_ref, w_ref, lg_ref, lb_ref, ws_ref, bst_ref, y_ref):
    tm = x_ref.shape[0]
    width = y_ref.shape[1]
    dg = width // SGU_GROUPS
    n_chunks = tm // SGU_CHUNK
    hn = _rms(x_ref[...], g_ref[...]).astype(BF16)
    u = jax.nn.gelu(jnp.dot(hn, w_ref[:, :width], preferred_element_type=F32), approximate=True)
    v = jax.nn.gelu(jnp.dot(hn, w_ref[:, width:], preferred_element_type=F32), approximate=True)
    mu = jnp.mean(v, axis=-1, keepdims=True)
    vc = v - mu
    var = jnp.mean(vc * vc, axis=-1, keepdims=True)
    v = (vc * lax.rsqrt(var + EPS) * lg_ref[...] + lb_ref[...]).astype(BF16)
    bst = bst_ref[...]
    for gi in range(SGU_GROUPS):
        cols = slice(gi * dg, (gi + 1) * dg)
        vg = jnp.concatenate(
            [v[c * SGU_CHUNK:(c + 1) * SGU_CHUNK, cols] for c in range(n_chunks)], axis=1)
        mixed = jnp.dot(ws_ref[gi], vg, preferred_element_type=F32) + bst[:, gi:gi + 1]
        for c in range(n_chunks):
            rows = slice(c * SGU_CHUNK, (c + 1) * SGU_CHUNK)
            y_ref[rows, cols] = (u[rows, cols] * mixed[:, c * dg:(c + 1) * dg]).astype(BF16)


def _sgu(x2d, g, w_in, ln_g, ln_b, w_s, b_st):
    t_rows, d = x2d.shape
    width = w_in.shape[1] // 2
    tm = TOKEN_TILE
    vec = lambda c: pl.BlockSpec((1, c), lambda i: (0, 0))
    return pl.pallas_call(
        _sgu_kernel,
        grid=(t_rows // tm,),
        in_specs=[pl.BlockSpec((tm, d), lambda i: (i, 0)), vec(d), _resident(w_in.shape),
                  vec(width), vec(width), _resident(w_s.shape), _resident(b_st.shape)],
        out_specs=pl.BlockSpec((tm, width), lambda i: (i, 0)),
        out_shape=jax.ShapeDtypeStruct((t_rows, width), BF16),
        compiler_params=_params(("parallel",)),
        name="sgu",
    )(x2d, g, w_in, ln_g, ln_b, w_s, b_st)


def _pair_order(n_heads):
    half = n_heads // 2
    return [h for p in range(half) for h in (p, half + p)]


def _rope_tables(s_len):
    def angles(pos, dim):
        freqs = ROPE_THETA ** (-jnp.arange(0, dim, 2, dtype=F32) / dim)
        ang = pos.astype(F32)[:, None] * freqs[None, :]
        return jnp.cos(ang), jnp.sin(ang)

    pos = jnp.arange(s_len)
    cos1, sin1 = angles(pos, HEAD_DIM)
    cos_r, sin_r = angles(pos // GRID_W, HEAD_DIM // 2)
    cos_c, sin_c = angles(pos % GRID_W, HEAD_DIM // 2)
    ca = jnp.tile(cos1, (1, 4))
    sa = jnp.tile(jnp.concatenate([-sin1, sin1], axis=1), (1, 2))
    cb = jnp.tile(jnp.concatenate([cos_r, cos_r, cos_c, cos_c], axis=1), (1, 2))
    sb = jnp.tile(jnp.concatenate([-sin_r, sin_r, -sin_c, sin_c], axis=1), (1, 2))
    return ca, sa, cb, sb


def kernel(x, att_norm, att_w_in, att_sink, att_qnorm, att_knorm, att_w_out, sgu_norm, sgu_w_in,
           sgu_ln_g, sgu_ln_b, sgu_w_s, sgu_b_s, sgu_w_out, mlp_norm, mlp_w1, mlp_w2, final_norm):
    bsz, s_len, d = x.shape
    depth = mlp_w1.shape[0]
    n_q = 2 * N_PAIRS
    qw = n_q * HEAD_DIM
    kvw = 2 * HEAD_DIM
    assert s_len % (ATT_TQ * WIN_STEP_TILES) == 0 and s_len % GRID_TK == 0 and GRID_TK % TOKEN_TILE == 0
    assert n_q % N_SLOTS == 0 and TOKEN_TILE % ATT_TQ == 0

    order = np.asarray(_pair_order(n_q))
    q_cols = (order[:, None] * HEAD_DIM + np.arange(HEAD_DIM)[None, :]).reshape(-1)
    offs_b = qw + 2 * kvw
    in_cols = np.concatenate([q_cols, np.arange(qw, offs_b), offs_b + q_cols,
                              np.arange(offs_b + qw, offs_b + qw + 2 * kvw)])
    out_rows = np.concatenate([q_cols, qw + q_cols])

    tabs = _rope_tables(s_len)
    lane_head = np.arange(PAIR_W) // HEAD_DIM
    seg = jnp.asarray(lane_head[:, None] == lane_head[None, :], BF16)
    row = lambda v: v.reshape(1, -1).astype(F32)

    h = x.reshape(bsz * s_len, d)
    for layer in range(depth):
        i = layer // 2
        if layer % 2 == 0:
            w_in = att_w_in[i][:, in_cols].astype(BF16)
            w_out = att_w_out[i][out_rows, :].astype(BF16)
            qn = row(jnp.tile(att_qnorm[i], 2))
            kn = row(jnp.tile(att_knorm[i], 2))
            qat, ka, vat, qbt, kb, vbt = _attn_in(h, row(att_norm[i]), w_in, tabs, qn, kn, seg,
                                                  bsz, s_len)
            r3 = lambda a: a.reshape(bsz, s_len, a.shape[-1])
            oa = _window_attn(att_sink[i].astype(F32), qat, r3(ka), vat)
            ob = _grid_attn(qbt, r3(kb), vbt)
            ys = [oa.reshape(bsz * s_len, qw), ob.reshape(bsz * s_len, qw)]
            wo = w_out
        else:
            y = _sgu(h, row(sgu_norm[i]), sgu_w_in[i].astype(BF16), row(sgu_ln_g[i]),
                     row(sgu_ln_b[i]), sgu_w_s[i].astype(BF16), sgu_b_s[i].T.astype(F32))
            ys = [y]
            wo = sgu_w_out[i].astype(BF16)
        h = _proj_mlp(h, ys, wo, row(mlp_norm[layer]), mlp_w1[layer].astype(BF16),
                      mlp_w2[layer].astype(BF16), row(final_norm), final=(layer == depth - 1))
    return h.reshape(bsz, s_len, d)
```

```python
import functools

import numpy as np
import jax
import jax.numpy as jnp
from jax import lax
from jax.experimental import pallas as pl
from jax.experimental.pallas import tpu as pltpu

F32 = jnp.float32
BF16 = jnp.bfloat16

HEAD_DIM = 64
N_PAIRS = 4
PAIR_W = 2 * HEAD_DIM
WINDOW = 128
BLOCK = 128
ROPE_THETA = 10000.0
GRID_W = 64
SGU_GROUPS = 8
SGU_CHUNK = 128
EPS = 1e-6
NEG_BIG = -1e30
LOG2E = 1.4426950408889634
Q_SCALE = HEAD_DIM ** -0.5 * LOG2E

VMEM_LIMIT_BYTES = 56 * 1024 * 1024

TOKEN_TILE = 512
ATT_TQ = 256
WIN_KEYS = ATT_TQ + 2 * BLOCK
WIN_STEP_TILES = 8
GRID_TK = 1024
GRID_STEP_TILES = 4
PIPE_DEPTH = 3
N_SLOTS = PIPE_DEPTH + 1
ONES_ROWS = 16
VT_ROWS = HEAD_DIM + ONES_ROWS
FF_CHUNK = 1024


def _params(sem):
    return pltpu.CompilerParams(dimension_semantics=sem, vmem_limit_bytes=VMEM_LIMIT_BYTES)


def _resident(shape):
    nd = len(shape)
    return pl.BlockSpec(shape, lambda *_: (0,) * nd, pipeline_mode=pl.Buffered(1))


def _rms(x, g):
    ms = jnp.mean(x * x, axis=-1, keepdims=True)
    return x * lax.rsqrt(ms + EPS) * g


def _with_ones(vt):
    ones = jnp.ones((ONES_ROWS, vt.shape[1]), F32)
    return jnp.concatenate([vt[:HEAD_DIM], ones, vt[HEAD_DIM:], ones], axis=0).astype(BF16)


def _attn_in_kernel(x_ref, g_ref, w_ref, ca_ref, sa_ref, cb_ref, sb_ref, qn_ref, kn_ref, seg_ref,
                    qat_ref, ka_ref, vat_ref, qbt_ref, kb_ref, vbt_ref):
    tm = x_ref.shape[0]
    hn = _rms(x_ref[...], g_ref[...]).astype(BF16)
    proj = jnp.dot(hn, w_ref[...], preferred_element_type=F32)

    lane = lax.broadcasted_iota(jnp.int32, (tm, PAIR_W), 1)
    first32 = (lane % HEAD_DIM) < (HEAD_DIM // 2)
    first16 = (lane % (HEAD_DIM // 2)) < (HEAD_DIM // 4)
    ca, sa, cb, sb = ca_ref[...], sa_ref[...], cb_ref[...], sb_ref[...]
    seg = seg_ref[...]

    def rope_a(t):
        partner = jnp.where(first32, pltpu.roll(t, PAIR_W - 32, 1), pltpu.roll(t, 32, 1))
        return t * ca + partner * sa

    def rope_b(t):
        partner = jnp.where(first16, pltpu.roll(t, PAIR_W - 16, 1), pltpu.roll(t, 16, 1))
        return t * cb + partner * sb

    def headnorm(t, gain):
        t2 = t * t
        hi = t2.astype(BF16)
        lo = (t2 - hi.astype(F32)).astype(BF16)
        ss = (jnp.dot(hi, seg, preferred_element_type=F32)
              + jnp.dot(lo, seg, preferred_element_type=F32))
        return t * lax.rsqrt(ss * (1.0 / HEAD_DIM) + EPS) * gain

    def blk(i):
        return proj[:, i * PAIR_W:(i + 1) * PAIR_W]

    for p in range(N_PAIRS):
        t = (rope_a(blk(p)) * Q_SCALE).T.astype(BF16)
        for c in range(tm // ATT_TQ):
            qat_ref[0, c, p * PAIR_W:(p + 1) * PAIR_W, :] = t[:, c * ATT_TQ:(c + 1) * ATT_TQ]
    ka_ref[...] = rope_a(blk(4)).astype(BF16)
    vat = _with_ones(blk(5).T)
    for c in range(tm // BLOCK):
        vat_ref[0, c] = vat[:, c * BLOCK:(c + 1) * BLOCK]
    qn, kn = qn_ref[...], kn_ref[...]
    for p in range(N_PAIRS):
        t = rope_b(headnorm(blk(6 + p), qn)) * Q_SCALE
        qbt_ref[0, p * PAIR_W:(p + 1) * PAIR_W, :] = t.T.astype(BF16)
    kb_ref[...] = rope_b(headnorm(blk(10), kn)).astype(BF16)
    vbt_ref[0, 0] = _with_ones(blk(11).T)


def _attn_in(x2d, g, w, tabs, qn, kn, seg, bsz, s_len):
    t_rows, d = x2d.shape
    tm = TOKEN_TILE
    n_s = s_len // tm
    n_cols = w.shape[1]
    qw = N_PAIRS * PAIR_W
    r_q, r_b, r_k = tm // ATT_TQ, tm // BLOCK, GRID_TK // tm
    row_blk = lambda c: pl.BlockSpec((tm, c), lambda i: (i, 0))
    tab_blk = pl.BlockSpec((tm, PAIR_W), lambda i: (i % n_s, 0))
    vec = lambda c: pl.BlockSpec((1, c), lambda i: (0, 0))
    out_shape = (
        jax.ShapeDtypeStruct((bsz, s_len // ATT_TQ, qw, ATT_TQ), BF16),
        jax.ShapeDtypeStruct((t_rows, PAIR_W), BF16),
        jax.ShapeDtypeStruct((bsz, s_len // BLOCK, 2 * VT_ROWS, BLOCK), BF16),
        jax.ShapeDtypeStruct((bsz, qw, s_len), BF16),
        jax.ShapeDtypeStruct((t_rows, PAIR_W), BF16),
        jax.ShapeDtypeStruct((bsz, s_len // GRID_TK, 2 * VT_ROWS, GRID_TK), BF16),
    )
    out_specs = (
        pl.BlockSpec((1, r_q, qw, ATT_TQ), lambda i: (i // n_s, i % n_s, 0, 0)),
        row_blk(PAIR_W),
        pl.BlockSpec((1, r_b, 2 * VT_ROWS, BLOCK), lambda i: (i // n_s, i % n_s, 0, 0)),
        pl.BlockSpec((1, qw, tm), lambda i: (i // n_s, 0, i % n_s)),
        row_blk(PAIR_W),
        pl.BlockSpec((1, 1, 2 * VT_ROWS, tm), lambda i: (i // n_s, (i % n_s) // r_k, 0, (i % n_s) % r_k)),
    )
    return pl.pallas_call(
        _attn_in_kernel,
        grid=(t_rows // tm,),
        in_specs=[row_blk(d), vec(d), _resident((d, n_cols)),
                  tab_blk, tab_blk, tab_blk, tab_blk,
                  vec(PAIR_W), vec(PAIR_W), _resident((PAIR_W, PAIR_W))],
        out_specs=out_specs,
        out_shape=out_shape,
        compiler_params=_params(("parallel",)),
        name="attn_in",
    )(x2d, g, w, *tabs, qn, kn, seg)


def _pipelined(n_outer, n_heads, scores, update):
    def steps(o, wrap):
        for h in range(n_heads):
            nh = h + PIPE_DEPTH
            if nh < n_heads:
                scores(o, nh)
            elif wrap:
                scores(o + 1, nh - n_heads)
            update(o, h)

    for d in range(PIPE_DEPTH):
        scores(0, d)

    def body(o, carry):
        steps(o, True)
        return carry

    lax.fori_loop(0, n_outer - 1, body, 0)
    steps(n_outer - 1, False)


def _masked_q(qt, h):
    row = lax.broadcasted_iota(jnp.int32, qt.shape, 0)
    keep = (row < HEAD_DIM) if h % 2 == 0 else (row >= HEAD_DIM)
    return jnp.where(keep, qt, jnp.zeros_like(qt))


def _window_kernel(sink_ref, qt_ref, k_ref, vt_ref, o_ref, bias_ref, s_ref, m_ref, ot_ref, *, s_len):
    n_tiles, qw, tq = qt_ref.shape[1:]
    n_heads = 2 * N_PAIRS
    n_blk = s_len // BLOCK
    blk_per_tile = tq // BLOCK
    n_kblk = WIN_KEYS // BLOCK
    step = pl.program_id(1)
    kj = lax.broadcasted_iota(jnp.int32, (WIN_KEYS, tq), 0)
    qi = lax.broadcasted_iota(jnp.int32, (WIN_KEYS, tq), 1)
    band = jnp.abs(kj - BLOCK - qi) <= WINDOW

    def first_block(t):
        return (step * n_tiles + t) * blk_per_tile - 1

    def key_blocks(t):
        b0 = first_block(t)
        return [jnp.clip(b0 + c, 0, n_blk - 1) for c in range(n_kblk)]

    def scores(t, h):
        if h == 0:
            kpos = first_block(t) * BLOCK + kj
            ok = band & (kpos >= 0) & (kpos < s_len)
            bias_ref[...] = jnp.where(ok, 0.0, NEG_BIG)
        kband = jnp.concatenate(
            [k_ref[0, pl.ds(pl.multiple_of(b * BLOCK, BLOCK), BLOCK), :] for b in key_blocks(t)], axis=0)
        p = h // 2
        qx = _masked_q(qt_ref[0, t, p * PAIR_W:(p + 1) * PAIR_W, :], h)
        s_t = jnp.dot(kband, qx, preferred_element_type=F32) + bias_ref[...]
        s_ref[h % N_SLOTS] = s_t
        sink = sink_ref[(h % 2) * N_PAIRS + p] * LOG2E
        m_ref[h % N_SLOTS] = jnp.maximum(jnp.max(s_t, axis=0, keepdims=True), sink)

    def update(t, h):
        slot = h % N_SLOTS
        m = m_ref[slot]
        p_t = jnp.exp2(s_ref[slot] - m).astype(BF16)
        rows = slice((h % 2) * VT_ROWS, (h % 2 + 1) * VT_ROWS)
        v_t = jnp.concatenate([vt_ref[0, b, rows, :] for b in key_blocks(t)], axis=1)
        r = jnp.dot(v_t, p_t, preferred_element_type=F32)
        sink = sink_ref[(h % 2) * N_PAIRS + h // 2] * LOG2E
        denom = r[HEAD_DIM:HEAD_DIM + 1] + jnp.exp2(sink - m)
        ot_ref[h * HEAD_DIM:(h + 1) * HEAD_DIM, :] = r[:HEAD_DIM] / denom
        if h == n_heads - 1:
            o_ref[0, pl.ds(pl.multiple_of(t * tq, tq), tq), :] = ot_ref[...].T.astype(BF16)

    _pipelined(n_tiles, n_heads, scores, update)


def _window_attn(sink, qat, ka, vat):
    bsz, n_qt, qw, tq = qat.shape
    s_len = ka.shape[1]
    n_t = WIN_STEP_TILES
    return pl.pallas_call(
        functools.partial(_window_kernel, s_len=s_len),
        grid=(bsz, n_qt // n_t),
        in_specs=[pl.BlockSpec(memory_space=pltpu.SMEM),
                  pl.BlockSpec((1, n_t, qw, tq), lambda b, i: (b, i, 0, 0)),
                  pl.BlockSpec((1, s_len, PAIR_W), lambda b, i: (b, 0, 0)),
                  pl.BlockSpec((1,) + vat.shape[1:], lambda b, i: (b, 0, 0, 0))],
        out_specs=pl.BlockSpec((1, n_t * tq, qw), lambda b, i: (b, i, 0)),
        out_shape=jax.ShapeDtypeStruct((bsz, s_len, qw), BF16),
        scratch_shapes=[pltpu.VMEM((WIN_KEYS, tq), F32),
                        pltpu.VMEM((N_SLOTS, WIN_KEYS, tq), F32),
                        pltpu.VMEM((N_SLOTS, 1, tq), F32),
                        pltpu.VMEM((qw, tq), F32)],
        compiler_params=_params(("parallel", "arbitrary")),
        name="window_attn",
    )(sink, qat, ka, vat)


def _grid_kernel(qt_ref, k_ref, vt_ref, o_ref, qx_ref, m_ref, acc_ref, s_ref, al_ref):
    tq = ATT_TQ
    n_kc, _, tk = vt_ref.shape[1:]
    n_heads = 2 * N_PAIRS
    n_items = qx_ref.shape[0]
    for it in range(n_items):
        h, qs = it % n_heads, it // n_heads
        p = h // 2
        qx_ref[it] = _masked_q(qt_ref[0, p * PAIR_W:(p + 1) * PAIR_W, qs * tq:(qs + 1) * tq], h)
    m_ref[...] = jnp.full(m_ref.shape, NEG_BIG, F32)
    acc_ref[...] = jnp.zeros(acc_ref.shape, F32)

    def scores(kk, it):
        slot = it % N_SLOTS
        kblk = k_ref[0, pl.ds(pl.multiple_of(kk * tk, tk), tk), :]
        s_t = jnp.dot(kblk, qx_ref[it], preferred_element_type=F32)
        s_ref[slot] = s_t
        m_old = m_ref[it]
        m_new = jnp.maximum(m_old, jnp.max(s_t, axis=0, keepdims=True))
        al_ref[slot] = jnp.exp2(m_old - m_new)
        m_ref[it] = m_new

    def update(kk, it):
        slot = it % N_SLOTS
        half = it % 2
        p_t = jnp.exp2(s_ref[slot] - m_ref[it]).astype(BF16)
        v_t = vt_ref[0, kk, half * VT_ROWS:(half + 1) * VT_ROWS, :]
        acc_ref[it] = al_ref[slot] * acc_ref[it] + jnp.dot(v_t, p_t, preferred_element_type=F32)

    _pipelined(n_kc, n_items, scores, update)

    for qs in range(n_items // n_heads):
        outs = []
        for h in range(n_heads):
            a = acc_ref[qs * n_heads + h]
            outs.append(a[:HEAD_DIM] / a[HEAD_DIM:HEAD_DIM + 1])
        o_ref[0, qs * tq:(qs + 1) * tq, :] = jnp.concatenate(outs, axis=0).T.astype(BF16)


def _grid_attn(qbt, kb, vbt):
    bsz, qw, s_len = qbt.shape
    n_kc, vrows, tk = vbt.shape[1:]
    tq = ATT_TQ * GRID_STEP_TILES
    n_items = 2 * N_PAIRS * GRID_STEP_TILES
    return pl.pallas_call(
        _grid_kernel,
        grid=(bsz, s_len // tq),
        in_specs=[pl.BlockSpec((1, qw, tq), lambda b, i: (b, 0, i)),
                  pl.BlockSpec((1, s_len, PAIR_W), lambda b, i: (b, 0, 0)),
                  pl.BlockSpec((1, n_kc, vrows, tk), lambda b, i: (b, 0, 0, 0))],
        out_specs=pl.BlockSpec((1, tq, qw), lambda b, i: (b, i, 0)),
        out_shape=jax.ShapeDtypeStruct((bsz, s_len, qw), BF16),
        scratch_shapes=[pltpu.VMEM((n_items, PAIR_W, ATT_TQ), BF16),
                        pltpu.VMEM((n_items, 1, ATT_TQ), F32),
                        pltpu.VMEM((n_items, VT_ROWS, ATT_TQ), F32),
                        pltpu.VMEM((N_SLOTS, tk, ATT_TQ), F32),
                        pltpu.VMEM((N_SLOTS, 1, ATT_TQ), F32)],
        compiler_params=_params(("parallel", "arbitrary")),
        name="grid_attn",
    )(qbt, kb, vbt)


def _proj_mlp_kernel(*refs, n_y, final):
    x_ref = refs[0]
    y_refs = refs[1:1 + n_y]
    wo_ref, g_ref, w1_ref, w2_ref, gf_ref, o_ref = refs[1 + n_y:]
    y = y_refs[0][...] if n_y == 1 else jnp.concatenate([r[...] for r in y_refs], axis=1)
    x1 = x_ref[...] + jnp.dot(y, wo_ref[...], preferred_element_type=F32)
    hn = _rms(x1, g_ref[...]).astype(BF16)
    acc = x1
    d_ff = w1_ref.shape[1]
    for c in range(d_ff // FF_CHUNK):
        h = jnp.dot(hn, w1_ref[:, c * FF_CHUNK:(c + 1) * FF_CHUNK], preferred_element_type=F32)
        h = jnp.maximum(h, 0.0)
        acc = acc + jnp.dot((h * h).astype(BF16), w2_ref[c * FF_CHUNK:(c + 1) * FF_CHUNK, :],
                            preferred_element_type=F32)
    if final:
        acc = _rms(acc, gf_ref[...])
    o_ref[...] = acc


def _proj_mlp(x2d, ys, wo, g, w1, w2, gf, final):
    t_rows, d = x2d.shape
    tm = TOKEN_TILE
    row_blk = lambda c: pl.BlockSpec((tm, c), lambda i: (i, 0))
    vec = pl.BlockSpec((1, d), lambda i: (0, 0))
    in_specs = ([row_blk(d)] + [row_blk(y.shape[1]) for y in ys]
                + [_resident(wo.shape), vec, _resident(w1.shape), _resident(w2.shape), vec])
    return pl.pallas_call(
        functools.partial(_proj_mlp_kernel, n_y=len(ys), final=final),
        grid=(t_rows // tm,),
        in_specs=in_specs,
        out_specs=row_blk(d),
        out_shape=jax.ShapeDtypeStruct((t_rows, d), F32),
        compiler_params=_params(("parallel",)),
        name="proj_mlp",
    )(x2d, *ys, wo, g, w1, w2, gf)


def _sgu_kernel(x_ref, g_ref, w_ref, lg_ref, lb_ref, ws_ref, bst_ref, y_ref):
    tm = x_ref.shape[0]
    width = y_ref.shape[1]
    dg = width // SGU_GROUPS
    n_chunks = tm // SGU_CHUNK
    hn = _rms(x_ref[...], g_ref[...]).astype(BF16)
    u = jax.nn.gelu(jnp.dot(hn, w_ref[:, :width], preferred_element_type=F32), approximate=True)
    v = jax.nn.gelu(jnp.dot(hn, w_ref[:, width:], preferred_element_type=F32), approximate=True)
    mu = jnp.mean(v, axis=-1, keepdims=True)
    vc = v - mu
    var = jnp.mean(vc * vc, axis=-1, keepdims=True)
    v = (vc * lax.rsqrt(var + EPS) * lg_ref[...] + lb_ref[...]).astype(BF16)
    bst = bst_ref[...]
    for gi in range(SGU_GROUPS):
        cols = slice(gi * dg, (gi + 1) * dg)
        vg = jnp.concatenate(
            [v[c * SGU_CHUNK:(c + 1) * SGU_CHUNK, cols] for c in range(n_chunks)], axis=1)
        mixed = jnp.dot(ws_ref[gi], vg, preferred_element_type=F32) + bst[:, gi:gi + 1]
        for c in range(n_chunks):
            rows = slice(c * SGU_CHUNK, (c + 1) * SGU_CHUNK)
            y_ref[rows, cols] = (u[rows, cols] * mixed[:, c * dg:(c + 1) * dg]).astype(BF16)


def _sgu(x2d, g, w_in, ln_g, ln_b, w_s, b_st):
    t_rows, d = x2d.shape
    width = w_in.shape[1] // 2
    tm = TOKEN_TILE
    vec = lambda c: pl.BlockSpec((1, c), lambda i: (0, 0))
    return pl.pallas_call(
        _sgu_kernel,
        grid=(t_rows // tm,),
        in_specs=[pl.BlockSpec((tm, d), lambda i: (i, 0)), vec(d), _resident(w_in.shape),
                  vec(width), vec(width), _resident(w_s.shape), _resident(b_st.shape)],
        out_specs=pl.BlockSpec((tm, width), lambda i: (i, 0)),
        out_shape=jax.ShapeDtypeStruct((t_rows, width), BF16),
        compiler_params=_params(("parallel",)),
        name="sgu",
    )(x2d, g, w_in, ln_g, ln_b, w_s, b_st)


def _pair_order(n_heads):
    half = n_heads // 2
    return [h for p in range(half) for h in (p, half + p)]


def _rope_tables(s_len):
    def angles(pos, dim):
        freqs = ROPE_THETA ** (-jnp.arange(0, dim, 2, dtype=F32) / dim)
        ang = pos.astype(F32)[:, None] * freqs[None, :]
        return jnp.cos(ang), jnp.sin(ang)

    pos = jnp.arange(s_len)
    cos1, sin1 = angles(pos, HEAD_DIM)
    cos_r, sin_r = angles(pos // GRID_W, HEAD_DIM // 2)
    cos_c, sin_c = angles(pos % GRID_W, HEAD_DIM // 2)
    ca = jnp.tile(cos1, (1, 4))
    sa = jnp.tile(jnp.concatenate([-sin1, sin1], axis=1), (1, 2))
    cb = jnp.tile(jnp.concatenate([cos_r, cos_r, cos_c, cos_c], axis=1), (1, 2))
    sb = jnp.tile(jnp.concatenate([-sin_r, sin_r, -sin_c, sin_c], axis=1), (1, 2))
    return ca, sa, cb, sb


def kernel(x, att_norm, att_w_in, att_sink, att_qnorm, att_knorm, att_w_out, sgu_norm, sgu_w_in,
           sgu_ln_g, sgu_ln_b, sgu_w_s, sgu_b_s, sgu_w_out, mlp_norm, mlp_w1, mlp_w2, final_norm):
    bsz, s_len, d = x.shape
    depth = mlp_w1.shape[0]
    n_q = 2 * N_PAIRS
    qw = n_q * HEAD_DIM
    kvw = 2 * HEAD_DIM
    assert s_len % (ATT_TQ * WIN_STEP_TILES) == 0 and s_len % GRID_TK == 0 and GRID_TK % TOKEN_TILE == 0
    assert n_q % N_SLOTS == 0 and TOKEN_TILE % ATT_TQ == 0 and s_len % (ATT_TQ * GRID_STEP_TILES) == 0

    order = np.asarray(_pair_order(n_q))
    q_cols = (order[:, None] * HEAD_DIM + np.arange(HEAD_DIM)[None, :]).reshape(-1)
    offs_b = qw + 2 * kvw
    in_cols = np.concatenate([q_cols, np.arange(qw, offs_b), offs_b + q_cols,
                              np.arange(offs_b + qw, offs_b + qw + 2 * kvw)])
    out_rows = np.concatenate([q_cols, qw + q_cols])

    tabs = _rope_tables(s_len)
    lane_head = np.arange(PAIR_W) // HEAD_DIM
    seg = jnp.asarray(lane_head[:, None] == lane_head[None, :], BF16)
    row = lambda v: v.reshape(1, -1).astype(F32)

    h = x.reshape(bsz * s_len, d)
    for layer in range(depth):
        i = layer // 2
        if layer % 2 == 0:
            w_in = att_w_in[i][:, in_cols].astype(BF16)
            w_out = att_w_out[i][out_rows, :].astype(BF16)
            qn = row(jnp.tile(att_qnorm[i], 2))
            kn = row(jnp.tile(att_knorm[i], 2))
            qat, ka, vat, qbt, kb, vbt = _attn_in(h, row(att_norm[i]), w_in, tabs, qn, kn, seg,
                                                  bsz, s_len)
            r3 = lambda a: a.reshape(bsz, s_len, a.shape[-1])
            oa = _window_attn(att_sink[i].astype(F32), qat, r3(ka), vat)
            ob = _grid_attn(qbt, r3(kb), vbt)
            ys = [oa.reshape(bsz * s_len, qw), ob.reshape(bsz * s_len, qw)]
            wo = w_out
        else:
            y = _sgu(h, row(sgu_norm[i]), sgu_w_in[i].astype(BF16), row(sgu_ln_g[i]),
                     row(sgu_ln_b[i]), sgu_w_s[i].astype(BF16), sgu_b_s[i].T.astype(F32))
            ys = [y]
            wo = sgu_w_out[i].astype(BF16)
        h = _proj_mlp(h, ys, wo, row(mlp_norm[layer]), mlp_w1[layer].astype(BF16),
                      mlp_w2[layer].astype(BF16), row(final_norm), final=(layer == depth - 1))
    return h.reshape(bsz, s_len, d)
```

```python
import functools

import numpy as np
import jax
import jax.numpy as jnp
from jax import lax
from jax.experimental import pallas as pl
from jax.experimental.pallas import tpu as pltpu

F32 = jnp.float32
BF16 = jnp.bfloat16

HEAD_DIM = 64
N_PAIRS = 4
PAIR_W = 2 * HEAD_DIM
WINDOW = 128
BLOCK = 128
ROPE_THETA = 10000.0
GRID_W = 64
SGU_GROUPS = 8
SGU_CHUNK = 128
EPS = 1e-6
NEG_BIG = -1e30
LOG2E = 1.4426950408889634
Q_SCALE = HEAD_DIM ** -0.5 * LOG2E

VMEM_LIMIT_BYTES = 56 * 1024 * 1024

TOKEN_TILE = 1024
ATT_TQ = 256
WIN_KEYS = ATT_TQ + 2 * BLOCK
WIN_STEP_TILES = 8
GRID_TK = 1024
GRID_STEP_TILES = 4
PIPE_DEPTH = 3
N_SLOTS = PIPE_DEPTH + 1
ONES_ROWS = 16
VT_ROWS = HEAD_DIM + ONES_ROWS
FF_CHUNK = 1024


def _params(sem):
    return pltpu.CompilerParams(dimension_semantics=sem, vmem_limit_bytes=VMEM_LIMIT_BYTES)


def _resident(shape):
    nd = len(shape)
    return pl.BlockSpec(shape, lambda *_: (0,) * nd, pipeline_mode=pl.Buffered(1))


def _rms(x, g):
    ms = jnp.mean(x * x, axis=-1, keepdims=True)
    return x * lax.rsqrt(ms + EPS) * g


def _with_ones(vt):
    ones = jnp.ones((ONES_ROWS, vt.shape[1]), F32)
    return jnp.concatenate([vt[:HEAD_DIM], ones, vt[HEAD_DIM:], ones], axis=0).astype(BF16)


def _attn_in_kernel(x_ref, g_ref, w_ref, ca_ref, sa_ref, cb_ref, sb_ref, qn_ref, kn_ref, seg_ref,
                    qat_ref, ka_ref, vat_ref, qbt_ref, kb_ref, vbt_ref):
    tm = x_ref.shape[0]
    hn = _rms(x_ref[...], g_ref[...]).astype(BF16)
    proj = jnp.dot(hn, w_ref[...], preferred_element_type=F32)

    ca, sa, cb, sb = ca_ref[...], sa_ref[...], cb_ref[...], sb_ref[...]
    seg = seg_ref[...]

    def rope(t, cos, sin):
        return t * cos + pltpu.roll(t, PAIR_W // 2, 1) * sin

    def headnorm(t, gain):
        t2 = t * t
        hi = t2.astype(BF16)
        lo = (t2 - hi.astype(F32)).astype(BF16)
        ss = (jnp.dot(hi, seg, preferred_element_type=F32)
              + jnp.dot(lo, seg, preferred_element_type=F32))
        return t * lax.rsqrt(ss * (1.0 / HEAD_DIM) + EPS) * gain

    def blk(i):
        return proj[:, i * PAIR_W:(i + 1) * PAIR_W]

    for p in range(N_PAIRS):
        t = (rope(blk(p), ca, sa) * Q_SCALE).T.astype(BF16)
        for c in range(tm // ATT_TQ):
            qat_ref[0, c, p * PAIR_W:(p + 1) * PAIR_W, :] = t[:, c * ATT_TQ:(c + 1) * ATT_TQ]
    ka_ref[...] = rope(blk(4), ca, sa).astype(BF16)
    vat = _with_ones(blk(5).T)
    for c in range(tm // BLOCK):
        vat_ref[0, c] = vat[:, c * BLOCK:(c + 1) * BLOCK]
    qn, kn = qn_ref[...], kn_ref[...]
    for p in range(N_PAIRS):
        t = rope(headnorm(blk(6 + p), qn), cb, sb) * Q_SCALE
        qbt_ref[0, p * PAIR_W:(p + 1) * PAIR_W, :] = t.T.astype(BF16)
    kb_ref[...] = rope(headnorm(blk(10), kn), cb, sb).astype(BF16)
    vbt_ref[0, 0] = _with_ones(blk(11).T)


def _attn_in(x2d, g, w, tabs, qn, kn, seg, bsz, s_len):
    t_rows, d = x2d.shape
    tm = TOKEN_TILE
    n_s = s_len // tm
    n_cols = w.shape[1]
    qw = N_PAIRS * PAIR_W
    r_q, r_b, r_k = tm // ATT_TQ, tm // BLOCK, GRID_TK // tm
    row_blk = lambda c: pl.BlockSpec((tm, c), lambda i: (i, 0))
    tab_blk = pl.BlockSpec((tm, PAIR_W), lambda i: (i % n_s, 0))
    vec = lambda c: pl.BlockSpec((1, c), lambda i: (0, 0))
    out_shape = (
        jax.ShapeDtypeStruct((bsz, s_len // ATT_TQ, qw, ATT_TQ), BF16),
        jax.ShapeDtypeStruct((t_rows, PAIR_W), BF16),
        jax.ShapeDtypeStruct((bsz, s_len // BLOCK, 2 * VT_ROWS, BLOCK), BF16),
        jax.ShapeDtypeStruct((bsz, qw, s_len), BF16),
        jax.ShapeDtypeStruct((t_rows, PAIR_W), BF16),
        jax.ShapeDtypeStruct((bsz, s_len // GRID_TK, 2 * VT_ROWS, GRID_TK), BF16),
    )
    out_specs = (
        pl.BlockSpec((1, r_q, qw, ATT_TQ), lambda i: (i // n_s, i % n_s, 0, 0)),
        row_blk(PAIR_W),
        pl.BlockSpec((1, r_b, 2 * VT_ROWS, BLOCK), lambda i: (i // n_s, i % n_s, 0, 0)),
        pl.BlockSpec((1, qw, tm), lambda i: (i // n_s, 0, i % n_s)),
        row_blk(PAIR_W),
        pl.BlockSpec((1, 1, 2 * VT_ROWS, tm), lambda i: (i // n_s, (i % n_s) // r_k, 0, (i % n_s) % r_k)),
    )
    return pl.pallas_call(
        _attn_in_kernel,
        grid=(t_rows // tm,),
        in_specs=[row_blk(d), vec(d), _resident((d, n_cols)),
                  tab_blk, tab_blk, tab_blk, tab_blk,
                  vec(PAIR_W), vec(PAIR_W), _resident((PAIR_W, PAIR_W))],
        out_specs=out_specs,
        out_shape=out_shape,
        compiler_params=_params(("parallel",)),
        name="attn_in",
    )(x2d, g, w, *tabs, qn, kn, seg)


def _pipelined(n_outer, n_heads, scores, update):
    def steps(o, wrap):
        for h in range(n_heads):
            nh = h + PIPE_DEPTH
            if nh < n_heads:
                scores(o, nh)
            elif wrap:
                scores(o + 1, nh - n_heads)
            update(o, h)

    for d in range(PIPE_DEPTH):
        scores(0, d)

    def body(o, carry):
        steps(o, True)
        return carry

    lax.fori_loop(0, n_outer - 1, body, 0)
    steps(n_outer - 1, False)


def _masked_q(qt, h):
    row = lax.broadcasted_iota(jnp.int32, qt.shape, 0)
    group0 = (row % HEAD_DIM) < (HEAD_DIM // 2)
    keep = group0 if h % 2 == 0 else jnp.logical_not(group0)
    return jnp.where(keep, qt, jnp.zeros_like(qt))


def _window_kernel(sink_ref, qt_ref, k_ref, vt_ref, o_ref, bias_ref, s_ref, m_ref, ot_ref, *, s_len):
    n_tiles, qw, tq = qt_ref.shape[1:]
    n_heads = 2 * N_PAIRS
    n_blk = s_len // BLOCK
    blk_per_tile = tq // BLOCK
    n_kblk = WIN_KEYS // BLOCK
    step = pl.program_id(1)
    kj = lax.broadcasted_iota(jnp.int32, (WIN_KEYS, tq), 0)
    qi = lax.broadcasted_iota(jnp.int32, (WIN_KEYS, tq), 1)
    band = jnp.abs(kj - BLOCK - qi) <= WINDOW

    def first_block(t):
        return (step * n_tiles + t) * blk_per_tile - 1

    def key_blocks(t):
        b0 = first_block(t)
        return [jnp.clip(b0 + c, 0, n_blk - 1) for c in range(n_kblk)]

    def scores(t, h):
        if h == 0:
            kpos = first_block(t) * BLOCK + kj
            ok = band & (kpos >= 0) & (kpos < s_len)
            bias_ref[...] = jnp.where(ok, 0.0, NEG_BIG)
        kband = jnp.concatenate(
            [k_ref[0, pl.ds(pl.multiple_of(b * BLOCK, BLOCK), BLOCK), :] for b in key_blocks(t)], axis=0)
        p = h // 2
        qx = _masked_q(qt_ref[0, t, p * PAIR_W:(p + 1) * PAIR_W, :], h)
        s_t = jnp.dot(kband, qx, preferred_element_type=F32) + bias_ref[...]
        s_ref[h % N_SLOTS] = s_t
        sink = sink_ref[(h % 2) * N_PAIRS + p] * LOG2E
        m_ref[h % N_SLOTS] = jnp.maximum(jnp.max(s_t, axis=0, keepdims=True), sink)

    def update(t, h):
        slot = h % N_SLOTS
        m = m_ref[slot]
        p_t = jnp.exp2(s_ref[slot] - m).astype(BF16)
        rows = slice((h % 2) * VT_ROWS, (h % 2 + 1) * VT_ROWS)
        v_t = jnp.concatenate([vt_ref[0, b, rows, :] for b in key_blocks(t)], axis=1)
        r = jnp.dot(v_t, p_t, preferred_element_type=F32)
        sink = sink_ref[(h % 2) * N_PAIRS + h // 2] * LOG2E
        denom = r[HEAD_DIM:HEAD_DIM + 1] + jnp.exp2(sink - m)
        ot_ref[h * HEAD_DIM:(h + 1) * HEAD_DIM, :] = r[:HEAD_DIM] / denom
        if h == n_heads - 1:
            o_ref[0, pl.ds(pl.multiple_of(t * tq, tq), tq), :] = ot_ref[...].T.astype(BF16)

    _pipelined(n_tiles, n_heads, scores, update)


def _window_attn(sink, qat, ka, vat):
    bsz, n_qt, qw, tq = qat.shape
    s_len = ka.shape[1]
    n_t = WIN_STEP_TILES
    return pl.pallas_call(
        functools.partial(_window_kernel, s_len=s_len),
        grid=(bsz, n_qt // n_t),
        in_specs=[pl.BlockSpec(memory_space=pltpu.SMEM),
                  pl.BlockSpec((1, n_t, qw, tq), lambda b, i: (b, i, 0, 0)),
                  pl.BlockSpec((1, s_len, PAIR_W), lambda b, i: (b, 0, 0)),
                  pl.BlockSpec((1,) + vat.shape[1:], lambda b, i: (b, 0, 0, 0))],
        out_specs=pl.BlockSpec((1, n_t * tq, qw), lambda b, i: (b, i, 0)),
        out_shape=jax.ShapeDtypeStruct((bsz, s_len, qw), BF16),
        scratch_shapes=[pltpu.VMEM((WIN_KEYS, tq), F32),
                        pltpu.VMEM((N_SLOTS, WIN_KEYS, tq), F32),
                        pltpu.VMEM((N_SLOTS, 1, tq), F32),
                        pltpu.VMEM((qw, tq), F32)],
        compiler_params=_params(("parallel", "arbitrary")),
        name="window_attn",
    )(sink, qat, ka, vat)


def _grid_kernel(qt_ref, k_ref, vt_ref, o_ref, qx_ref, m_ref, acc_ref, s_ref, al_ref):
    tq = ATT_TQ
    n_kc, _, tk = vt_ref.shape[1:]
    n_heads = 2 * N_PAIRS
    n_items = qx_ref.shape[0]
    for it in range(n_items):
        h, qs = it % n_heads, it // n_heads
        p = h // 2
        qx_ref[it] = _masked_q(qt_ref[0, p * PAIR_W:(p + 1) * PAIR_W, qs * tq:(qs + 1) * tq], h)
    m_ref[...] = jnp.full(m_ref.shape, NEG_BIG, F32)
    acc_ref[...] = jnp.zeros(acc_ref.shape, F32)

    def scores(kk, it):
        slot = it % N_SLOTS
        kblk = k_ref[0, pl.ds(pl.multiple_of(kk * tk, tk), tk), :]
        s_t = jnp.dot(kblk, qx_ref[it], preferred_element_type=F32)
        s_ref[slot] = s_t
        m_old = m_ref[it]
        m_new = jnp.maximum(m_old, jnp.max(s_t, axis=0, keepdims=True))
        al_ref[slot] = jnp.exp2(m_old - m_new)
        m_ref[it] = m_new

    def update(kk, it):
        slot = it % N_SLOTS
        half = it % 2
        p_t = jnp.exp2(s_ref[slot] - m_ref[it]).astype(BF16)
        v_t = vt_ref[0, kk, half * VT_ROWS:(half + 1) * VT_ROWS, :]
        acc_ref[it] = al_ref[slot] * acc_ref[it] + jnp.dot(v_t, p_t, preferred_element_type=F32)

    _pipelined(n_kc, n_items, scores, update)

    for qs in range(n_items // n_heads):
        outs = []
        for h in range(n_heads):
            a = acc_ref[qs * n_heads + h]
            outs.append(a[:HEAD_DIM] / a[HEAD_DIM:HEAD_DIM + 1])
        o_ref[0, qs * tq:(qs + 1) * tq, :] = jnp.concatenate(outs, axis=0).T.astype(BF16)


def _grid_attn(qbt, kb, vbt):
    bsz, qw, s_len = qbt.shape
    n_kc, vrows, tk = vbt.shape[1:]
    tq = ATT_TQ * GRID_STEP_TILES
    n_items = 2 * N_PAIRS * GRID_STEP_TILES
    return pl.pallas_call(
        _grid_kernel,
        grid=(bsz, s_len // tq),
        in_specs=[pl.BlockSpec((1, qw, tq), lambda b, i: (b, 0, i)),
                  pl.BlockSpec((1, s_len, PAIR_W), lambda b, i: (b, 0, 0)),
                  pl.BlockSpec((1, n_kc, vrows, tk), lambda b, i: (b, 0, 0, 0))],
        out_specs=pl.BlockSpec((1, tq, qw), lambda b, i: (b, i, 0)),
        out_shape=jax.ShapeDtypeStruct((bsz, s_len, qw), BF16),
        scratch_shapes=[pltpu.VMEM((n_items, PAIR_W, ATT_TQ), BF16),
                        pltpu.VMEM((n_items, 1, ATT_TQ), F32),
                        pltpu.VMEM((n_items, VT_ROWS, ATT_TQ), F32),
                        pltpu.VMEM((N_SLOTS, tk, ATT_TQ), F32),
                        pltpu.VMEM((N_SLOTS, 1, ATT_TQ), F32)],
        compiler_params=_params(("parallel", "arbitrary")),
        name="grid_attn",
    )(qbt, kb, vbt)


def _proj_mlp_kernel(*refs, n_y, final):
    x_ref = refs[0]
    y_refs = refs[1:1 + n_y]
    wo_ref, g_ref, w1_ref, w2_ref, gf_ref, o_ref = refs[1 + n_y:]
    y = y_refs[0][...] if n_y == 1 else jnp.concatenate([r[...] for r in y_refs], axis=1)
    x1 = x_ref[...] + jnp.dot(y, wo_ref[...], preferred_element_type=F32)
    hn = _rms(x1, g_ref[...]).astype(BF16)
    acc = x1
    d_ff = w1_ref.shape[1]
    for c in range(d_ff // FF_CHUNK):
        h = jnp.dot(hn, w1_ref[:, c * FF_CHUNK:(c + 1) * FF_CHUNK], preferred_element_type=F32)
        h = jnp.maximum(h, 0.0)
        acc = acc + jnp.dot((h * h).astype(BF16), w2_ref[c * FF_CHUNK:(c + 1) * FF_CHUNK, :],
                            preferred_element_type=F32)
    if final:
        acc = _rms(acc, gf_ref[...])
    o_ref[...] = acc


def _proj_mlp(x2d, ys, wo, g, w1, w2, gf, final):
    t_rows, d = x2d.shape
    tm = TOKEN_TILE
    row_blk = lambda c: pl.BlockSpec((tm, c), lambda i: (i, 0))
    vec = pl.BlockSpec((1, d), lambda i: (0, 0))
    in_specs = ([row_blk(d)] + [row_blk(y.shape[1]) for y in ys]
                + [_resident(wo.shape), vec, _resident(w1.shape), _resident(w2.shape), vec])
    return pl.pallas_call(
        functools.partial(_proj_mlp_kernel, n_y=len(ys), final=final),
        grid=(t_rows // tm,),
        in_specs=in_specs,
        out_specs=row_blk(d),
        out_shape=jax.ShapeDtypeStruct((t_rows, d), F32),
        compiler_params=_params(("parallel",)),
        name="proj_mlp",
    )(x2d, *ys, wo, g, w1, w2, gf)


def _sgu_kernel(x_ref, g_ref, w_ref, lg_ref, lb_ref, ws_ref, bst_ref, y_ref):
    tm = x_ref.shape[0]
    width = y_ref.shape[1]
    dg = width // SGU_GROUPS
    n_chunks = tm // SGU_CHUNK
    hn = _rms(x_ref[...], g_ref[...]).astype(BF16)
    u = jax.nn.gelu(jnp.dot(hn, w_ref[:, :width], preferred_element_type=F32), approximate=True)
    v = jax.nn.gelu(jnp.dot(hn, w_ref[:, width:], preferred_element_type=F32), approximate=True)
    mu = jnp.mean(v, axis=-1, keepdims=True)
    vc = v - mu
    var = jnp.mean(vc * vc, axis=-1, keepdims=True)
    v = (vc * lax.rsqrt(var + EPS) * lg_ref[...] + lb_ref[...]).astype(BF16)
    bst = bst_ref[...]
    for gi in range(SGU_GROUPS):
        cols = slice(gi * dg, (gi + 1) * dg)
        vg = jnp.concatenate(
            [v[c * SGU_CHUNK:(c + 1) * SGU_CHUNK, cols] for c in range(n_chunks)], axis=1)
        mixed = jnp.dot(ws_ref[gi], vg, preferred_element_type=F32) + bst[:, gi:gi + 1]
        for c in range(n_chunks):
            rows = slice(c * SGU_CHUNK, (c + 1) * SGU_CHUNK)
            y_ref[rows, cols] = (u[rows, cols] * mixed[:, c * dg:(c + 1) * dg]).astype(BF16)


def _sgu(x2d, g, w_in, ln_g, ln_b, w_s, b_st):
    t_rows, d = x2d.shape
    width = w_in.shape[1] // 2
    tm = TOKEN_TILE
    vec = lambda c: pl.BlockSpec((1, c), lambda i: (0, 0))
    return pl.pallas_call(
        _sgu_kernel,
        grid=(t_rows // tm,),
        in_specs=[pl.BlockSpec((tm, d), lambda i: (i, 0)), vec(d), _resident(w_in.shape),
                  vec(width), vec(width), _resident(w_s.shape), _resident(b_st.shape)],
        out_specs=pl.BlockSpec((tm, width), lambda i: (i, 0)),
        out_shape=jax.ShapeDtypeStruct((t_rows, width), BF16),
        compiler_params=_params(("parallel",)),
        name="sgu",
    )(x2d, g, w_in, ln_g, ln_b, w_s, b_st)


def _pair_order(n_heads):
    half = n_heads // 2
    return [h for p in range(half) for h in (p, half + p)]


def _rope_tables(s_len):
    def angles(pos, dim):
        freqs = ROPE_THETA ** (-np.arange(0, dim, 2, dtype=np.float64) / dim)
        ang = pos.astype(np.float64)[:, None] * freqs[None, :]
        return np.cos(ang), np.sin(ang)

    pos = np.arange(s_len)
    cos1, sin1 = angles(pos, HEAD_DIM)
    cos_r, sin_r = angles(pos // GRID_W, HEAD_DIM // 2)
    cos_c, sin_c = angles(pos % GRID_W, HEAD_DIM // 2)
    ca = np.tile(cos1, (1, 4))
    sa = np.concatenate([-sin1, -sin1, sin1, sin1], axis=1)
    cb = np.tile(np.concatenate([cos_r, cos_c], axis=1), (1, 4))
    sb = np.concatenate([-sin_r, -sin_c, -sin_r, -sin_c, sin_r, sin_c, sin_r, sin_c], axis=1)
    return tuple(jnp.asarray(t, F32) for t in (ca, sa, cb, sb))


def kernel(x, att_norm, att_w_in, att_sink, att_qnorm, att_knorm, att_w_out, sgu_norm, sgu_w_in,
           sgu_ln_g, sgu_ln_b, sgu_w_s, sgu_b_s, sgu_w_out, mlp_norm, mlp_w1, mlp_w2, final_norm):
    bsz, s_len, d = x.shape
    depth = mlp_w1.shape[0]
    n_q = 2 * N_PAIRS
    qw = n_q * HEAD_DIM
    kvw = 2 * HEAD_DIM
    assert s_len % (ATT_TQ * WIN_STEP_TILES) == 0 and s_len % GRID_TK == 0 and GRID_TK % TOKEN_TILE == 0
    assert n_q % N_SLOTS == 0 and TOKEN_TILE % ATT_TQ == 0 and s_len % (ATT_TQ * GRID_STEP_TILES) == 0

    lane = np.arange(PAIR_W)
    grp = (lane % HEAD_DIM) // (HEAD_DIM // 2)
    sub = lane % (HEAD_DIM // 2)
    second = lane // HEAD_DIM
    d_a = second * (HEAD_DIM // 2) + sub
    quarter = HEAD_DIM // 4
    d_b = np.where(sub < quarter, sub, 2 * quarter + (sub - quarter)) + second * quarter
    offs_b = qw + 2 * kvw

    def q_cols(base, dims):
        return np.concatenate([base + (grp * N_PAIRS + p) * HEAD_DIM + dims for p in range(N_PAIRS)])

    in_cols = np.concatenate([q_cols(0, d_a), qw + grp * HEAD_DIM + d_a, np.arange(qw + kvw, offs_b),
                              q_cols(offs_b, d_b), offs_b + qw + grp * HEAD_DIM + d_b,
                              np.arange(offs_b + qw + kvw, offs_b + qw + 2 * kvw)])
    order = np.asarray(_pair_order(n_q))
    o_cols = (order[:, None] * HEAD_DIM + np.arange(HEAD_DIM)[None, :]).reshape(-1)
    out_rows = np.concatenate([o_cols, qw + o_cols])

    tabs = _rope_tables(s_len)
    seg = jnp.asarray(grp[:, None] == grp[None, :], BF16)
    row = lambda v: v.reshape(1, -1).astype(F32)

    h = x.reshape(bsz * s_len, d)
    for layer in range(depth):
        i = layer // 2
        if layer % 2 == 0:
            w_in = att_w_in[i][:, in_cols].astype(BF16)
            w_out = att_w_out[i][out_rows, :].astype(BF16)
            qn = row(att_qnorm[i][d_b])
            kn = row(att_knorm[i][d_b])
            qat, ka, vat, qbt, kb, vbt = _attn_in(h, row(att_norm[i]), w_in, tabs, qn, kn, seg,
                                                  bsz, s_len)
            r3 = lambda a: a.reshape(bsz, s_len, a.shape[-1])
            oa = _window_attn(att_sink[i].astype(F32), qat, r3(ka), vat)
            ob = _grid_attn(qbt, r3(kb), vbt)
            ys = [oa.reshape(bsz * s_len, qw), ob.reshape(bsz * s_len, qw)]
            wo = w_out
        else:
            y = _sgu(h, row(sgu_norm[i]), sgu_w_in[i].astype(BF16), row(sgu_ln_g[i]),
                     row(sgu_ln_b[i]), sgu_w_s[i].astype(BF16), sgu_b_s[i].T.astype(F32))
            ys = [y]
            wo = sgu_w_out[i].astype(BF16)
        h = _proj_mlp(h, ys, wo, row(mlp_norm[layer]), mlp_w1[layer].astype(BF16),
                      mlp_w2[layer].astype(BF16), row(final_norm), final=(layer == depth - 1))
    return h.reshape(bsz, s_len, d)
```

```python
import functools

import numpy as np
import jax
import jax.numpy as jnp
from jax import lax
from jax.experimental import pallas as pl
from jax.experimental.pallas import tpu as pltpu

F32 = jnp.float32
BF16 = jnp.bfloat16

HEAD_DIM = 64
N_PAIRS = 4
PAIR_W = 2 * HEAD_DIM
WINDOW = 128
BLOCK = 128
ROPE_THETA = 10000.0
GRID_W = 64
SGU_GROUPS = 8
SGU_CHUNK = 128
EPS = 1e-6
NEG_BIG = -1e30
LOG2E = 1.4426950408889634
Q_SCALE = HEAD_DIM ** -0.5 * LOG2E

VMEM_LIMIT_BYTES = 56 * 1024 * 1024

TOKEN_TILE = 1024
ATT_TQ = 256
WIN_KEYS = ATT_TQ + 2 * BLOCK
WIN_STEP_TILES = 8
GRID_TK = 1024
GRID_STEP_TILES = 8
PIPE_DEPTH = 3
N_SLOTS = PIPE_DEPTH + 1
ONES_ROWS = 16
VT_ROWS = HEAD_DIM + ONES_ROWS
FF_CHUNK = 1024


def _params(sem):
    return pltpu.CompilerParams(dimension_semantics=sem, vmem_limit_bytes=VMEM_LIMIT_BYTES)


def _resident(shape):
    nd = len(shape)
    return pl.BlockSpec(shape, lambda *_: (0,) * nd, pipeline_mode=pl.Buffered(1))


def _rms(x, g):
    ms = jnp.mean(x * x, axis=-1, keepdims=True)
    return x * lax.rsqrt(ms + EPS) * g


def _with_ones(vt):
    ones = jnp.ones((ONES_ROWS, vt.shape[1]), F32)
    return jnp.concatenate([vt[:HEAD_DIM], ones, vt[HEAD_DIM:], ones], axis=0).astype(BF16)


def _attn_in_kernel(x_ref, g_ref, w_ref, ca_ref, sa_ref, cb_ref, sb_ref, qn_ref, kn_ref, seg_ref,
                    qat_ref, ka_ref, vat_ref, qbt_ref, kb_ref, vbt_ref):
    tm = x_ref.shape[0]
    hn = _rms(x_ref[...], g_ref[...]).astype(BF16)
    proj = jnp.dot(hn, w_ref[...], preferred_element_type=F32)

    ca, sa, cb, sb = ca_ref[...], sa_ref[...], cb_ref[...], sb_ref[...]
    seg = seg_ref[...]

    def rope(t, cos, sin):
        return t * cos + pltpu.roll(t, PAIR_W // 2, 1) * sin

    def headnorm(t, gain):
        t2 = t * t
        hi = t2.astype(BF16)
        lo = (t2 - hi.astype(F32)).astype(BF16)
        ss = (jnp.dot(hi, seg, preferred_element_type=F32)
              + jnp.dot(lo, seg, preferred_element_type=F32))
        return t * lax.rsqrt(ss * (1.0 / HEAD_DIM) + EPS) * gain

    def blk(i):
        return proj[:, i * PAIR_W:(i + 1) * PAIR_W]

    for p in range(N_PAIRS):
        t = (rope(blk(p), ca, sa) * Q_SCALE).T.astype(BF16)
        for c in range(tm // ATT_TQ):
            qat_ref[0, c, p * PAIR_W:(p + 1) * PAIR_W, :] = t[:, c * ATT_TQ:(c + 1) * ATT_TQ]
    ka_ref[...] = rope(blk(4), ca, sa).astype(BF16)
    vat = _with_ones(blk(5).T)
    for c in range(tm // BLOCK):
        vat_ref[0, c] = vat[:, c * BLOCK:(c + 1) * BLOCK]
    qn, kn = qn_ref[...], kn_ref[...]
    for p in range(N_PAIRS):
        t = rope(headnorm(blk(6 + p), qn), cb, sb) * Q_SCALE
        qbt_ref[0, p * PAIR_W:(p + 1) * PAIR_W, :] = t.T.astype(BF16)
    kb_ref[...] = rope(headnorm(blk(10), kn), cb, sb).astype(BF16)
    vbt_ref[0, 0] = _with_ones(blk(11).T)


def _attn_in(x2d, g, w, tabs, qn, kn, seg, bsz, s_len):
    t_rows, d = x2d.shape
    tm = TOKEN_TILE
    n_s = s_len // tm
    n_cols = w.shape[1]
    qw = N_PAIRS * PAIR_W
    r_q, r_b, r_k = tm // ATT_TQ, tm // BLOCK, GRID_TK // tm
    row_blk = lambda c: pl.BlockSpec((tm, c), lambda i: (i, 0))
    tab_blk = pl.BlockSpec((tm, PAIR_W), lambda i: (i % n_s, 0))
    vec = lambda c: pl.BlockSpec((1, c), lambda i: (0, 0))
    out_shape = (
        jax.ShapeDtypeStruct((bsz, s_len // ATT_TQ, qw, ATT_TQ), BF16),
        jax.ShapeDtypeStruct((t_rows, PAIR_W), BF16),
        jax.ShapeDtypeStruct((bsz, s_len // BLOCK, 2 * VT_ROWS, BLOCK), BF16),
        jax.ShapeDtypeStruct((bsz, qw, s_len), BF16),
        jax.ShapeDtypeStruct((t_rows, PAIR_W), BF16),
        jax.ShapeDtypeStruct((bsz, s_len // GRID_TK, 2 * VT_ROWS, GRID_TK), BF16),
    )
    out_specs = (
        pl.BlockSpec((1, r_q, qw, ATT_TQ), lambda i: (i // n_s, i % n_s, 0, 0)),
        row_blk(PAIR_W),
        pl.BlockSpec((1, r_b, 2 * VT_ROWS, BLOCK), lambda i: (i // n_s, i % n_s, 0, 0)),
        pl.BlockSpec((1, qw, tm), lambda i: (i // n_s, 0, i % n_s)),
        row_blk(PAIR_W),
        pl.BlockSpec((1, 1, 2 * VT_ROWS, tm), lambda i: (i // n_s, (i % n_s) // r_k, 0, (i % n_s) % r_k)),
    )
    return pl.pallas_call(
        _attn_in_kernel,
        grid=(t_rows // tm,),
        in_specs=[row_blk(d), vec(d), _resident((d, n_cols)),
                  tab_blk, tab_blk, tab_blk, tab_blk,
                  vec(PAIR_W), vec(PAIR_W), _resident((PAIR_W, PAIR_W))],
        out_specs=out_specs,
        out_shape=out_shape,
        compiler_params=_params(("parallel",)),
        name="attn_in",
    )(x2d, g, w, *tabs, qn, kn, seg)


def _pipelined(n_outer, n_heads, scores, update):
    def steps(o, wrap):
        for h in range(n_heads):
            nh = h + PIPE_DEPTH
            if nh < n_heads:
                scores(o, nh)
            elif wrap:
                scores(o + 1, nh - n_heads)
            update(o, h)

    for d in range(PIPE_DEPTH):
        scores(0, d)

    def body(o, carry):
        steps(o, True)
        return carry

    lax.fori_loop(0, n_outer - 1, body, 0)
    steps(n_outer - 1, False)


def _masked_q(qt, h):
    row = lax.broadcasted_iota(jnp.int32, qt.shape, 0)
    group0 = (row % HEAD_DIM) < (HEAD_DIM // 2)
    keep = group0 if h % 2 == 0 else jnp.logical_not(group0)
    return jnp.where(keep, qt, jnp.zeros_like(qt))


def _window_kernel(sink_ref, qt_ref, k_ref, vt_ref, o_ref, bias_ref, s_ref, m_ref, ot_ref, *, s_len):
    n_tiles, qw, tq = qt_ref.shape[1:]
    n_heads = 2 * N_PAIRS
    n_blk = s_len // BLOCK
    blk_per_tile = tq // BLOCK
    n_kblk = WIN_KEYS // BLOCK
    step = pl.program_id(1)
    kj = lax.broadcasted_iota(jnp.int32, (WIN_KEYS, tq), 0)
    qi = lax.broadcasted_iota(jnp.int32, (WIN_KEYS, tq), 1)
    band = jnp.abs(kj - BLOCK - qi) <= WINDOW

    def first_block(t):
        return (step * n_tiles + t) * blk_per_tile - 1

    def key_blocks(t):
        b0 = first_block(t)
        return [jnp.clip(b0 + c, 0, n_blk - 1) for c in range(n_kblk)]

    def scores(t, h):
        if h == 0:
            kpos = first_block(t) * BLOCK + kj
            ok = band & (kpos >= 0) & (kpos < s_len)
            bias_ref[...] = jnp.where(ok, 0.0, NEG_BIG)
        kband = jnp.concatenate(
            [k_ref[0, pl.ds(pl.multiple_of(b * BLOCK, BLOCK), BLOCK), :] for b in key_blocks(t)], axis=0)
        p = h // 2
        qx = _masked_q(qt_ref[0, t, p * PAIR_W:(p + 1) * PAIR_W, :], h)
        s_t = jnp.dot(kband, qx, preferred_element_type=F32) + bias_ref[...]
        s_ref[h % N_SLOTS] = s_t
        sink = sink_ref[(h % 2) * N_PAIRS + p] * LOG2E
        m_ref[h % N_SLOTS] = jnp.maximum(jnp.max(s_t, axis=0, keepdims=True), sink)

    def update(t, h):
        slot = h % N_SLOTS
        m = m_ref[slot]
        p_t = jnp.exp2(s_ref[slot] - m).astype(BF16)
        rows = slice((h % 2) * VT_ROWS, (h % 2 + 1) * VT_ROWS)
        v_t = jnp.concatenate([vt_ref[0, b, rows, :] for b in key_blocks(t)], axis=1)
        r = jnp.dot(v_t, p_t, preferred_element_type=F32)
        sink = sink_ref[(h % 2) * N_PAIRS + h // 2] * LOG2E
        denom = r[HEAD_DIM:HEAD_DIM + 1] + jnp.exp2(sink - m)
        ot_ref[h * HEAD_DIM:(h + 1) * HEAD_DIM, :] = r[:HEAD_DIM] / denom
        if h == n_heads - 1:
            o_ref[0, pl.ds(pl.multiple_of(t * tq, tq), tq), :] = ot_ref[...].T.astype(BF16)

    _pipelined(n_tiles, n_heads, scores, update)


def _window_attn(sink, qat, ka, vat):
    bsz, n_qt, qw, tq = qat.shape
    s_len = ka.shape[1]
    n_t = WIN_STEP_TILES
    return pl.pallas_call(
        functools.partial(_window_kernel, s_len=s_len),
        grid=(bsz, n_qt // n_t),
        in_specs=[pl.BlockSpec(memory_space=pltpu.SMEM),
                  pl.BlockSpec((1, n_t, qw, tq), lambda b, i: (b, i, 0, 0)),
                  pl.BlockSpec((1, s_len, PAIR_W), lambda b, i: (b, 0, 0)),
                  pl.BlockSpec((1,) + vat.shape[1:], lambda b, i: (b, 0, 0, 0))],
        out_specs=pl.BlockSpec((1, n_t * tq, qw), lambda b, i: (b, i, 0)),
        out_shape=jax.ShapeDtypeStruct((bsz, s_len, qw), BF16),
        scratch_shapes=[pltpu.VMEM((WIN_KEYS, tq), F32),
                        pltpu.VMEM((N_SLOTS, WIN_KEYS, tq), F32),
                        pltpu.VMEM((N_SLOTS, 1, tq), F32),
                        pltpu.VMEM((qw, tq), F32)],
        compiler_params=_params(("parallel", "arbitrary")),
        name="window_attn",
    )(sink, qat, ka, vat)


def _grid_kernel(qt_ref, k_ref, vt_ref, o_ref, qx_ref, m_ref, acc_ref, s_ref, al_ref):
    tq = ATT_TQ
    n_kc, _, tk = vt_ref.shape[1:]
    n_heads = 2 * N_PAIRS
    n_items = qx_ref.shape[0]
    for it in range(n_items):
        h, qs = it % n_heads, it // n_heads
        p = h // 2
        qx_ref[it] = _masked_q(qt_ref[0, p * PAIR_W:(p + 1) * PAIR_W, qs * tq:(qs + 1) * tq], h)
    m_ref[...] = jnp.full(m_ref.shape, NEG_BIG, F32)
    acc_ref[...] = jnp.zeros(acc_ref.shape, F32)

    def scores(kk, it):
        slot = it % N_SLOTS
        kblk = k_ref[0, pl.ds(pl.multiple_of(kk * tk, tk), tk), :]
        s_t = jnp.dot(kblk, qx_ref[it], preferred_element_type=F32)
        s_ref[slot] = s_t
        m_old = m_ref[it]
        m_new = jnp.maximum(m_old, jnp.max(s_t, axis=0, keepdims=True))
        al_ref[slot] = jnp.exp2(m_old - m_new)
        m_ref[it] = m_new

    def update(kk, it):
        slot = it % N_SLOTS
        half = it % 2
        p_t = jnp.exp2(s_ref[slot] - m_ref[it]).astype(BF16)
        v_t = vt_ref[0, kk, half * VT_ROWS:(half + 1) * VT_ROWS, :]
        acc_ref[it] = al_ref[slot] * acc_ref[it] + jnp.dot(v_t, p_t, preferred_element_type=F32)

    _pipelined(n_kc, n_items, scores, update)

    for qs in range(n_items // n_heads):
        outs = []
        for h in range(n_heads):
            a = acc_ref[qs * n_heads + h]
            outs.append(a[:HEAD_DIM] / a[HEAD_DIM:HEAD_DIM + 1])
        o_ref[0, qs * tq:(qs + 1) * tq, :] = jnp.concatenate(outs, axis=0).T.astype(BF16)


def _grid_attn(qbt, kb, vbt):
    bsz, qw, s_len = qbt.shape
    n_kc, vrows, tk = vbt.shape[1:]
    tq = ATT_TQ * GRID_STEP_TILES
    n_items = 2 * N_PAIRS * GRID_STEP_TILES
    return pl.pallas_call(
        _grid_kernel,
        grid=(bsz, s_len // tq),
        in_specs=[pl.BlockSpec((1, qw, tq), lambda b, i: (b, 0, i)),
                  pl.BlockSpec((1, s_len, PAIR_W), lambda b, i: (b, 0, 0)),
                  pl.BlockSpec((1, n_kc, vrows, tk), lambda b, i: (b, 0, 0, 0))],
        out_specs=pl.BlockSpec((1, tq, qw), lambda b, i: (b, i, 0)),
        out_shape=jax.ShapeDtypeStruct((bsz, s_len, qw), BF16),
        scratch_shapes=[pltpu.VMEM((n_items, PAIR_W, ATT_TQ), BF16),
                        pltpu.VMEM((n_items, 1, ATT_TQ), F32),
                        pltpu.VMEM((n_items, VT_ROWS, ATT_TQ), F32),
                        pltpu.VMEM((N_SLOTS, tk, ATT_TQ), F32),
                        pltpu.VMEM((N_SLOTS, 1, ATT_TQ), F32)],
        compiler_params=_params(("parallel", "arbitrary")),
        name="grid_attn",
    )(qbt, kb, vbt)


def _proj_mlp_kernel(*refs, n_y, final):
    x_ref = refs[0]
    y_refs = refs[1:1 + n_y]
    wo_ref, g_ref, w1_ref, w2_ref, gf_ref, o_ref = refs[1 + n_y:]
    y = y_refs[0][...] if n_y == 1 else jnp.concatenate([r[...] for r in y_refs], axis=1)
    x1 = x_ref[...] + jnp.dot(y, wo_ref[...], preferred_element_type=F32)
    hn = _rms(x1, g_ref[...]).astype(BF16)
    acc = x1
    d_ff = w1_ref.shape[1]
    for c in range(d_ff // FF_CHUNK):
        h = jnp.dot(hn, w1_ref[:, c * FF_CHUNK:(c + 1) * FF_CHUNK], preferred_element_type=F32)
        h = jnp.maximum(h, 0.0)
        acc = acc + jnp.dot((h * h).astype(BF16), w2_ref[c * FF_CHUNK:(c + 1) * FF_CHUNK, :],
                            preferred_element_type=F32)
    if final:
        acc = _rms(acc, gf_ref[...])
    o_ref[...] = acc


def _proj_mlp(x2d, ys, wo, g, w1, w2, gf, final):
    t_rows, d = x2d.shape
    tm = TOKEN_TILE
    row_blk = lambda c: pl.BlockSpec((tm, c), lambda i: (i, 0))
    vec = pl.BlockSpec((1, d), lambda i: (0, 0))
    in_specs = ([row_blk(d)] + [row_blk(y.shape[1]) for y in ys]
                + [_resident(wo.shape), vec, _resident(w1.shape), _resident(w2.shape), vec])
    return pl.pallas_call(
        functools.partial(_proj_mlp_kernel, n_y=len(ys), final=final),
        grid=(t_rows // tm,),
        in_specs=in_specs,
        out_specs=row_blk(d),
        out_shape=jax.ShapeDtypeStruct((t_rows, d), F32),
        compiler_params=_params(("parallel",)),
        name="proj_mlp",
    )(x2d, *ys, wo, g, w1, w2, gf)


def _sgu_kernel(x_ref, g_ref, w_ref, lg_ref, lb_ref, ws_ref, bst_ref, y_ref):
    tm = x_ref.shape[0]
    width = y_ref.shape[1]
    dg = width // SGU_GROUPS
    n_chunks = tm // SGU_CHUNK
    hn = _rms(x_ref[...], g_ref[...]).astype(BF16)
    u = jax.nn.gelu(jnp.dot(hn, w_ref[:, :width], preferred_element_type=F32), approximate=True)
    v = jax.nn.gelu(jnp.dot(hn, w_ref[:, width:], preferred_element_type=F32), approximate=True)
    mu = jnp.mean(v, axis=-1, keepdims=True)
    vc = v - mu
    var = jnp.mean(vc * vc, axis=-1, keepdims=True)
    v = (vc * lax.rsqrt(var + EPS) * lg_ref[...] + lb_ref[...]).astype(BF16)
    bst = bst_ref[...]
    for gi in range(SGU_GROUPS):
        cols = slice(gi * dg, (gi + 1) * dg)
        vg = jnp.concatenate(
            [v[c * SGU_CHUNK:(c + 1) * SGU_CHUNK, cols] for c in range(n_chunks)], axis=1)
        mixed = jnp.dot(ws_ref[gi], vg, preferred_element_type=F32) + bst[:, gi:gi + 1]
        for c in range(n_chunks):
            rows = slice(c * SGU_CHUNK, (c + 1) * SGU_CHUNK)
            y_ref[rows, cols] = (u[rows, cols] * mixed[:, c * dg:(c + 1) * dg]).astype(BF16)


def _sgu(x2d, g, w_in, ln_g, ln_b, w_s, b_st):
    t_rows, d = x2d.shape
    width = w_in.shape[1] // 2
    tm = TOKEN_TILE
    vec = lambda c: pl.BlockSpec((1, c), lambda i: (0, 0))
    return pl.pallas_call(
        _sgu_kernel,
        grid=(t_rows // tm,),
        in_specs=[pl.BlockSpec((tm, d), lambda i: (i, 0)), vec(d), _resident(w_in.shape),
                  vec(width), vec(width), _resident(w_s.shape), _resident(b_st.shape)],
        out_specs=pl.BlockSpec((tm, width), lambda i: (i, 0)),
        out_shape=jax.ShapeDtypeStruct((t_rows, width), BF16),
        compiler_params=_params(("parallel",)),
        name="sgu",
    )(x2d, g, w_in, ln_g, ln_b, w_s, b_st)


def _pair_order(n_heads):
    half = n_heads // 2
    return [h for p in range(half) for h in (p, half + p)]


def _rope_tables(s_len):
    def angles(pos, dim):
        freqs = ROPE_THETA ** (-np.arange(0, dim, 2, dtype=np.float64) / dim)
        ang = pos.astype(np.float64)[:, None] * freqs[None, :]
        return np.cos(ang), np.sin(ang)

    pos = np.arange(s_len)
    cos1, sin1 = angles(pos, HEAD_DIM)
    cos_r, sin_r = angles(pos // GRID_W, HEAD_DIM // 2)
    cos_c, sin_c = angles(pos % GRID_W, HEAD_DIM // 2)
    ca = np.tile(cos1, (1, 4))
    sa = np.concatenate([-sin1, -sin1, sin1, sin1], axis=1)
    cb = np.tile(np.concatenate([cos_r, cos_c], axis=1), (1, 4))
    sb = np.concatenate([-sin_r, -sin_c, -sin_r, -sin_c, sin_r, sin_c, sin_r, sin_c], axis=1)
    return tuple(jnp.asarray(t, F32) for t in (ca, sa, cb, sb))


def kernel(x, att_norm, att_w_in, att_sink, att_qnorm, att_knorm, att_w_out, sgu_norm, sgu_w_in,
           sgu_ln_g, sgu_ln_b, sgu_w_s, sgu_b_s, sgu_w_out, mlp_norm, mlp_w1, mlp_w2, final_norm):
    bsz, s_len, d = x.shape
    depth = mlp_w1.shape[0]
    n_q = 2 * N_PAIRS
    qw = n_q * HEAD_DIM
    kvw = 2 * HEAD_DIM
    assert s_len % (ATT_TQ * WIN_STEP_TILES) == 0 and s_len % GRID_TK == 0 and GRID_TK % TOKEN_TILE == 0
    assert n_q % N_SLOTS == 0 and TOKEN_TILE % ATT_TQ == 0 and s_len % (ATT_TQ * GRID_STEP_TILES) == 0

    lane = np.arange(PAIR_W)
    grp = (lane % HEAD_DIM) // (HEAD_DIM // 2)
    sub = lane % (HEAD_DIM // 2)
    second = lane // HEAD_DIM
    d_a = second * (HEAD_DIM // 2) + sub
    quarter = HEAD_DIM // 4
    d_b = np.where(sub < quarter, sub, 2 * quarter + (sub - quarter)) + second * quarter
    offs_b = qw + 2 * kvw

    def q_cols(base, dims):
        return np.concatenate([base + (grp * N_PAIRS + p) * HEAD_DIM + dims for p in range(N_PAIRS)])

    in_cols = np.concatenate([q_cols(0, d_a), qw + grp * HEAD_DIM + d_a, np.arange(qw + kvw, offs_b),
                              q_cols(offs_b, d_b), offs_b + qw + grp * HEAD_DIM + d_b,
                              np.arange(offs_b + qw + kvw, offs_b + qw + 2 * kvw)])
    order = np.asarray(_pair_order(n_q))
    o_cols = (order[:, None] * HEAD_DIM + np.arange(HEAD_DIM)[None, :]).reshape(-1)
    out_rows = np.concatenate([o_cols, qw + o_cols])

    tabs = _rope_tables(s_len)
    seg = jnp.asarray(grp[:, None] == grp[None, :], BF16)
    row = lambda v: v.reshape(1, -1).astype(F32)

    h = x.reshape(bsz * s_len, d)
    for layer in range(depth):
        i = layer // 2
        if layer % 2 == 0:
            w_in = att_w_in[i][:, in_cols].astype(BF16)
            w_out = att_w_out[i][out_rows, :].astype(BF16)
            qn = row(att_qnorm[i][d_b])
            kn = row(att_knorm[i][d_b])
            qat, ka, vat, qbt, kb, vbt = _attn_in(h, row(att_norm[i]), w_in, tabs, qn, kn, seg,
                                                  bsz, s_len)
            r3 = lambda a: a.reshape(bsz, s_len, a.shape[-1])
            oa = _window_attn(att_sink[i].astype(F32), qat, r3(ka), vat)
            ob = _grid_attn(qbt, r3(kb), vbt)
            ys = [oa.reshape(bsz * s_len, qw), ob.reshape(bsz * s_len, qw)]
            wo = w_out
        else:
            y = _sgu(h, row(sgu_norm[i]), sgu_w_in[i].astype(BF16), row(sgu_ln_g[i]),
                     row(sgu_ln_b[i]), sgu_w_s[i].astype(BF16), sgu_b_s[i].T.astype(F32))
            ys = [y]
            wo = sgu_w_out[i].astype(BF16)
        h = _proj_mlp(h, ys, wo, row(mlp_norm[layer]), mlp_w1[layer].astype(BF16),
                      mlp_w2[layer].astype(BF16), row(final_norm), final=(layer == depth - 1))
    return h.reshape(bsz, s_len, d)
```
